```python
import math
import jax, jax.numpy as jnp
from jax import lax
import numpy as np

D_MODEL = 1024
BATCH = 4
SEQ = 8192
DEPTH = 2

HEAD_DIM = 64
MEM_LEN = 256
MEM_HEADS = 4
MEM_WIDTH = MEM_HEADS * HEAD_DIM
MIX_WIDTH = D_MODEL
TOK_WIDTH = MIX_WIDTH - MEM_WIDTH
FOX_HEADS = TOK_WIDTH // HEAD_DIM
GMLP_GROUPS = TOK_WIDTH // HEAD_DIM
CHUNK = 128
Q_BLOCK = 128
D_FF = 2816
N_MIXERS = 2
N_FOX = (DEPTH + 1) // 2
N_GMLP = DEPTH // 2
FOX_IN = 3 * TOK_WIDTH + FOX_HEADS + MEM_WIDTH
GMLP_IN = 2 * TOK_WIDTH + MEM_WIDTH
EPS = 1e-6

kernel_name = "hybrid_fox_gmlp_macaron_memxattn"


def rms_norm(x, g):
    xf = x.astype(jnp.float32)
    y = xf * lax.rsqrt(jnp.mean(xf * xf, axis=-1, keepdims=True) + EPS)
    return (y * g.astype(jnp.float32)).astype(x.dtype)


def swiglu(h, w_in, w_out):
    a, b = jnp.split(h @ w_in, 2, axis=-1)
    return (jax.nn.silu(a) * b) @ w_out


def memory_attention(mq, mem_n, w_kv, g_q, g_k):
    b, s, _ = mq.shape
    q = rms_norm(mq.reshape(b, s, MEM_HEADS, HEAD_DIM), g_q)
    kv = mem_n @ w_kv
    k, v = jnp.split(kv, 2, axis=-1)
    k = rms_norm(k.reshape(b, MEM_LEN, MEM_HEADS, HEAD_DIM), g_k)
    v = v.reshape(b, MEM_LEN, MEM_HEADS, HEAD_DIM)
    logits = jnp.einsum('bshd,bmhd->bhsm', q, k).astype(jnp.float32) / math.sqrt(HEAD_DIM)
    p = jax.nn.softmax(logits, axis=-1).astype(v.dtype)
    o = jnp.einsum('bhsm,bmhd->bshd', p, v)
    return o.reshape(b, s, MEM_WIDTH)


def forgetting_attention(q, k, v, c):
    b, s, h, d = q.shape
    nblk = s // Q_BLOCK
    qb = q.reshape(b, nblk, Q_BLOCK, h, d).transpose(1, 0, 2, 3, 4)
    cq = c.reshape(b, h, nblk, Q_BLOCK).transpose(2, 0, 1, 3)
    starts = jnp.arange(nblk, dtype=jnp.int32) * Q_BLOCK
    key_pos = jnp.arange(s, dtype=jnp.int32)
    scale = 1.0 / math.sqrt(d)

    def block(args):
        q_i, c_i, start = args
        logits = jnp.einsum('bqhd,bkhd->bhqk', q_i, k).astype(jnp.float32) * scale
        logits = logits + c_i[:, :, :, None] - c[:, :, None, :]
        q_pos = start + jnp.arange(Q_BLOCK, dtype=jnp.int32)
        mask = key_pos[None, :] <= q_pos[:, None]
        logits = jnp.where(mask[None, None], logits, -jnp.inf)
        p = jax.nn.softmax(logits, axis=-1).astype(v.dtype)
        return jnp.einsum('bhqk,bkhd->bqhd', p, v)

    out = lax.map(block, (qb, cq, starts))
    return out.transpose(1, 0, 2, 3, 4).reshape(b, s, h * d)


def fox_token_mixer(proj, b_f, g_q, g_k):
    b, s, _ = proj.shape
    t = TOK_WIDTH
    q = rms_norm(proj[..., :t].reshape(b, s, FOX_HEADS, HEAD_DIM), g_q)
    k = rms_norm(proj[..., t:2 * t].reshape(b, s, FOX_HEADS, HEAD_DIM), g_k)
    v = proj[..., 2 * t:3 * t].reshape(b, s, FOX_HEADS, HEAD_DIM)
    f_logit = proj[..., 3 * t:3 * t + FOX_HEADS].astype(jnp.float32) + b_f.astype(jnp.float32)
    log_f = jax.nn.log_sigmoid(f_logit)
    c = jnp.cumsum(log_f, axis=1).transpose(0, 2, 1)
    mq = proj[..., 3 * t + FOX_HEADS:]
    return forgetting_attention(q, k, v, c), mq


def gmlp_token_mixer(proj, v_gain, w_s, b_s):
    b, s, _ = proj.shape
    t = TOK_WIDTH
    z = jax.nn.gelu(proj[..., :2 * t])
    u, v = z[..., :t], z[..., t:]
    v = rms_norm(v.reshape(b, s, GMLP_GROUPS, HEAD_DIM), v_gain.reshape(GMLP_GROUPS, HEAD_DIM))
    n_chunk = s // CHUNK
    vc = v.reshape(b, n_chunk, CHUNK, GMLP_GROUPS, HEAD_DIM)
    w = jnp.tril(w_s)
    gate = jnp.einsum('gts,bcsgd->bctgd', w, vc) + b_s.T[None, None, :, :, None]
    out = u.reshape(b, n_chunk, CHUNK, GMLP_GROUPS, HEAD_DIM) * gate
    return out.reshape(b, s, t), proj[..., 2 * t:]


def setup_inputs(seed: int = 0) -> dict:
    key = jax.random.key(seed)
    ks = jax.random.split(key, 32)
    f32 = jnp.float32
    D, F = D_MODEL, D_FF

    def nrm(k, shape, scale):
        return jax.random.normal(k, shape, f32) * scale

    def gain(k, shape):
        return 1.0 + 0.02 * jax.random.normal(k, shape, f32)

    return {
        "x": jax.random.normal(ks[0], (BATCH, SEQ, D), f32),
        "mem": jax.random.normal(ks[1], (BATCH, MEM_LEN, D), f32),
        "norm_ffn1": gain(ks[2], (DEPTH, D)),
        "ffn1_w_in": nrm(ks[3], (DEPTH, D, 2 * F), D ** -0.5),
        "ffn1_w_out": nrm(ks[4], (DEPTH, F, D), F ** -0.5),
        "norm_mix": gain(ks[5], (DEPTH, D)),
        "norm_ffn2": gain(ks[6], (DEPTH, D)),
        "ffn2_w_in": nrm(ks[7], (DEPTH, D, 2 * F), D ** -0.5),
        "ffn2_w_out": nrm(ks[8], (DEPTH, F, D), F ** -0.5),
        "w_out": nrm(ks[9], (DEPTH, MIX_WIDTH, D), MIX_WIDTH ** -0.5),
        "mem_norm": gain(ks[10], (D,)),
        "mem_w_kv": nrm(ks[11], (DEPTH, D, 2 * MEM_WIDTH), D ** -0.5),
        "mem_q_norm": gain(ks[12], (DEPTH, HEAD_DIM)),
        "mem_k_norm": gain(ks[13], (DEPTH, HEAD_DIM)),
        "fox_w_in": nrm(ks[14], (N_FOX, D, FOX_IN), D ** -0.5),
        "fox_b_f": 2.0 + 4.0 * jax.random.uniform(ks[15], (N_FOX, FOX_HEADS), f32),
        "fox_q_norm": gain(ks[16], (N_FOX, HEAD_DIM)),
        "fox_k_norm": gain(ks[17], (N_FOX, HEAD_DIM)),
        "gmlp_w_in": nrm(ks[18], (N_GMLP, D, GMLP_IN), D ** -0.5),
        "gmlp_v_norm": gain(ks[19], (N_GMLP, TOK_WIDTH)),
        "gmlp_w_s": nrm(ks[20], (N_GMLP, GMLP_GROUPS, CHUNK, CHUNK), 0.5 * CHUNK ** -0.5),
        "gmlp_b_s": 1.0 + 0.02 * jax.random.normal(ks[21], (N_GMLP, GMLP_GROUPS, CHUNK), f32),
    }


def reference(x, mem, norm_ffn1, ffn1_w_in, ffn1_w_out, norm_mix, norm_ffn2, ffn2_w_in,
              ffn2_w_out, w_out, mem_norm, mem_w_kv, mem_q_norm, mem_k_norm, fox_w_in,
              fox_b_f, fox_q_norm, fox_k_norm, gmlp_w_in, gmlp_v_norm, gmlp_w_s, gmlp_b_s):
    mem_n = rms_norm(mem, mem_norm)
    for i in range(DEPTH):
        kind, j = i % N_MIXERS, i // N_MIXERS
        x = x + 0.5 * swiglu(rms_norm(x, norm_ffn1[i]), ffn1_w_in[i], ffn1_w_out[i])
        h = rms_norm(x, norm_mix[i])
        if kind == 0:
            tok, mq = fox_token_mixer(h @ fox_w_in[j], fox_b_f[j], fox_q_norm[j], fox_k_norm[j])
        else:
            tok, mq = gmlp_token_mixer(h @ gmlp_w_in[j], gmlp_v_norm[j], gmlp_w_s[j], gmlp_b_s[j])
        mo = memory_attention(mq, mem_n, mem_w_kv[i], mem_q_norm[i], mem_k_norm[i])
        x = x + jnp.concatenate([tok, mo], axis=-1) @ w_out[i]
        x = x + 0.5 * swiglu(rms_norm(x, norm_ffn2[i]), ffn2_w_in[i], ffn2_w_out[i])
    return x
```

```python
import functools
import math

import jax
import jax.numpy as jnp
from jax import lax
from jax.experimental import pallas as pl
from jax.experimental.pallas import tpu as pltpu

D_MODEL = 1024
D_FF = 2816
HEAD_DIM = 64
MEM_LEN = 256
MEM_HEADS = 4
MEM_WIDTH = MEM_HEADS * HEAD_DIM
TOK_WIDTH = D_MODEL - MEM_WIDTH
TOK_HEADS = TOK_WIDTH // HEAD_DIM
HEAD_PAIRS = TOK_HEADS // 2
CHUNK = 128
EPS = 1e-6

LANES = 128
PAIR_WIDTH = 2 * HEAD_DIM
GATE_ROWS = 16
NEG_BIG = -1e30
QK_SCALE = 1.0 / math.sqrt(HEAD_DIM)

FFN_TM = 512
FFN_FC = 256
MIX_TM = 512
ATT_TQ = 512
ATT_TK = 512
VMEM_LIMIT = 56 * 1024 * 1024

F32 = jnp.float32
BF16 = jnp.bfloat16


def _dot(a, b):
    return jnp.dot(a, b, preferred_element_type=F32)


def _dot_nt(a, b):
    return lax.dot_general(a, b, (((1,), (1,)), ((), ())), preferred_element_type=F32)


def _split2(y):
    hi = y.astype(BF16)
    lo = (y - hi.astype(F32)).astype(BF16)
    return hi, lo


def _split3(y):
    hi = y.astype(BF16)
    r = y - hi.astype(F32)
    mid = r.astype(BF16)
    lo = (r - mid.astype(F32)).astype(BF16)
    return hi, mid, lo


def _rms_rows(x, gain):
    ms = jnp.mean(x * x, axis=-1, keepdims=True)
    return x * lax.rsqrt(ms + EPS) * gain


def _group_rms(y, p_sum, p_bcast, gain):
    hi, lo = _split2(y * y)
    ss = _dot(hi, p_sum) + _dot(lo, p_sum)
    r = lax.rsqrt(ss * (1.0 / HEAD_DIM) + EPS)
    rh, rl = _split2(r)
    rb = _dot(rh, p_bcast) + _dot(rl, p_bcast)
    return y * rb * gain


def _group_mats(width):
    groups = width // HEAD_DIM
    lane = jnp.arange(width) // HEAD_DIM
    p_sum = (lane[:, None] == jnp.arange(LANES)[None, :]).astype(BF16)
    p_bcast = p_sum.T
    del groups
    return p_sum, p_bcast


def _const_spec(shape):
    nd = len(shape)
    return pl.BlockSpec(shape, lambda *_: (0,) * nd, pipeline_mode=pl.Buffered(1))


def _params(sem):
    return pltpu.CompilerParams(dimension_semantics=sem, vmem_limit_bytes=VMEM_LIMIT)


def _ffn_kernel(x_ref, g_ref, wa_ref, wb_ref, wo_ref, o_ref, h_scr, acc_scr):
    x = x_ref[...]
    h_scr[...] = _rms_rows(x, g_ref[...]).astype(BF16)
    acc_scr[...] = jnp.zeros_like(acc_scr)

    def body(c, carry):
        col = pl.multiple_of(c * FFN_FC, FFN_FC)
        h = h_scr[...]
        a = _dot(h, wa_ref[:, pl.ds(col, FFN_FC)])
        b = _dot(h, wb_ref[:, pl.ds(col, FFN_FC)])
        g = (a * jax.nn.sigmoid(a) * b).astype(BF16)
        acc_scr[...] += _dot(g, wo_ref[pl.ds(col, FFN_FC), :])
        return carry

    lax.fori_loop(0, D_FF // FFN_FC, body, 0)
    o_ref[...] = x + 0.5 * acc_scr[...]


def _ffn(x2, gain, w_in, w_out):
    n = x2.shape[0]
    wa = w_in[:, :D_FF].astype(BF16)
    wb = w_in[:, D_FF:].astype(BF16)
    wo = w_out.astype(BF16)
    return pl.pallas_call(
        _ffn_kernel,
        grid=(n // FFN_TM,),
        in_specs=[
            pl.BlockSpec((FFN_TM, D_MODEL), lambda i: (i, 0)),
            _const_spec((1, D_MODEL)),
            _const_spec((D_MODEL, D_FF)),
            _const_spec((D_MODEL, D_FF)),
            _const_spec((D_FF, D_MODEL)),
        ],
        out_specs=pl.BlockSpec((FFN_TM, D_MODEL), lambda i: (i, 0)),
        out_shape=jax.ShapeDtypeStruct((n, D_MODEL), F32),
        scratch_shapes=[pltpu.VMEM((FFN_TM, D_MODEL), BF16),
                        pltpu.VMEM((FFN_TM, D_MODEL), F32)],
        compiler_params=_params(("arbitrary",)),
        name="ffn",
    )(x2, gain.reshape(1, D_MODEL), wa, wb, wo)


def _fox_in_kernel(x_ref, g_ref, wqkv_ref, wft_ref, wmq_ref, bf_ref, gq_ref, gk_ref, gmq_ref,
                   ps_ref, pb_ref, psm_ref, pbm_ref, tri_ref,
                   q_ref, k_ref, v_ref, mq_ref, c_ref, carry_scr):
    @pl.when(pl.program_id(1) == 0)
    def _():
        carry_scr[...] = jnp.zeros_like(carry_scr)

    h = _rms_rows(x_ref[0], g_ref[...]).astype(BF16)
    t = TOK_WIDTH
    q = _dot(h, wqkv_ref[:, 0:t])
    q_ref[0] = (_group_rms(q, ps_ref[...], pb_ref[...], gq_ref[...]) * QK_SCALE).astype(BF16)
    k = _dot(h, wqkv_ref[:, t:2 * t])
    k_ref[0] = _group_rms(k, ps_ref[...], pb_ref[...], gk_ref[...]).astype(BF16)
    v_ref[0] = _dot(h, wqkv_ref[:, 2 * t:3 * t]).astype(BF16)
    mq = _dot(h, wmq_ref[...])
    mq_ref[0] = (_group_rms(mq, psm_ref[...], pbm_ref[...], gmq_ref[...]) * QK_SCALE).astype(BF16)

    f = _dot_nt(wft_ref[...], h) + bf_ref[:, 0:1]
    log_f = jnp.minimum(f, 0.0) - jnp.log1p(jnp.exp(-jnp.abs(f)))
    hi, mid, lo = _split3(log_f)
    tri = tri_ref[...]
    c = _dot(hi, tri) + _dot(mid, tri) + _dot(lo, tri) + carry_scr[:, 0:1]
    c_ref[0] = c
    carry_scr[...] = jnp.broadcast_to(c[:, MIX_TM - 1:MIX_TM], carry_scr.shape)


def _fox_in(x, gain, w_in, b_f, g_q, g_k, g_mq):
    b, s, _ = x.shape
    t = TOK_WIDTH
    wqkv = w_in[:, :3 * t].astype(BF16)
    wft = jnp.zeros((GATE_ROWS, D_MODEL), BF16).at[:TOK_HEADS].set(
        w_in[:, 3 * t:3 * t + TOK_HEADS].T.astype(BF16))
    wmq = w_in[:, 3 * t + TOK_HEADS:].astype(BF16)
    bf = jnp.zeros((GATE_ROWS, LANES), F32).at[:TOK_HEADS].set(
        jnp.broadcast_to(b_f[:, None], (TOK_HEADS, LANES)))
    ps, pb = _group_mats(t)
    psm, pbm = _group_mats(MEM_WIDTH)
    idx = jnp.arange(MIX_TM)
    tri = (idx[:, None] <= idx[None, :]).astype(BF16)
    tile = lambda w: pl.BlockSpec((1, MIX_TM, w), lambda bi, i: (bi, i, 0))
    return pl.pallas_call(
        _fox_in_kernel,
        grid=(b, s // MIX_TM),
        in_specs=[
            tile(D_MODEL),
            _const_spec((1, D_MODEL)),
            _const_spec((D_MODEL, 3 * t)),
            _const_spec((GATE_ROWS, D_MODEL)),
            _const_spec((D_MODEL, MEM_WIDTH)),
            _const_spec((GATE_ROWS, LANES)),
            _const_spec((1, t)),
            _const_spec((1, t)),
            _const_spec((1, MEM_WIDTH)),
            _const_spec((t, LANES)),
            _const_spec((LANES, t)),
            _const_spec((MEM_WIDTH, LANES)),
            _const_spec((LANES, MEM_WIDTH)),
            _const_spec((MIX_TM, MIX_TM)),
        ],
        out_specs=[tile(t), tile(t), tile(t), tile(MEM_WIDTH),
                   pl.BlockSpec((1, GATE_ROWS, MIX_TM), lambda bi, i: (bi, 0, i))],
        out_shape=[jax.ShapeDtypeStruct((b, s, t), BF16)] * 3
        + [jax.ShapeDtypeStruct((b, s, MEM_WIDTH), BF16),
           jax.ShapeDtypeStruct((b, GATE_ROWS, s), F32)],
        scratch_shapes=[pltpu.VMEM((GATE_ROWS, LANES), F32)],
        compiler_params=_params(("arbitrary", "arbitrary")),
        name="fox_in",
    )(x, gain.reshape(1, D_MODEL), wqkv, wft, wmq, bf,
      jnp.tile(g_q, TOK_HEADS).reshape(1, t), jnp.tile(g_k, TOK_HEADS).reshape(1, t),
      jnp.tile(g_mq, MEM_HEADS).reshape(1, MEM_WIDTH), ps, pb, psm, pbm, tri)


def _fox_attn_kernel(q_ref, k_ref, v_ref, c_ref, o_ref, qm_scr, m_scr, l_scr, acc_scr):
    i = pl.program_id(2)
    lane = lax.broadcasted_iota(jnp.int32, (ATT_TQ, PAIR_WIDTH), 1)
    first = lane < HEAD_DIM
    q = q_ref[0]
    zero = jnp.zeros_like(q)
    qm_scr[0] = jnp.where(first, q, zero)
    qm_scr[1] = jnp.where(first, zero, q)
    m_scr[...] = jnp.full_like(m_scr, NEG_BIG)
    l_scr[...] = jnp.zeros_like(l_scr)
    acc_scr[...] = jnp.zeros_like(acc_scr)
    q0 = pl.multiple_of(i * ATT_TQ, ATT_TQ)
    c_base = c_ref[0, 0, :, pl.ds(q0, ATT_TQ)][:, 0:1]

    def step(k0, masked):
        kb = k_ref[0, pl.ds(k0, ATT_TK), :]
        vb = v_ref[0, pl.ds(k0, ATT_TK), :]
        bias = c_base - c_ref[0, 0, :, pl.ds(k0, ATT_TK)]
        for hh in range(2):
            s = _dot_nt(qm_scr[hh], kb) + bias[hh:hh + 1, :]
            if masked:
                row = lax.broadcasted_iota(jnp.int32, (ATT_TQ, ATT_TK), 0)
                col = lax.broadcasted_iota(jnp.int32, (ATT_TQ, ATT_TK), 1)
                s = jnp.where(col <= row, s, NEG_BIG)
            m_prev = m_scr[hh]
            m_new = jnp.maximum(m_prev, jnp.max(s, axis=-1, keepdims=True))
            p = jnp.exp(s - m_new[:, 0:1])
            alpha = jnp.exp(m_prev - m_new)
            l_scr[hh] = alpha * l_scr[hh] + jnp.sum(p, axis=-1, keepdims=True)
            acc_scr[hh] = alpha * acc_scr[hh] + _dot(p.astype(BF16), vb)
            m_scr[hh] = m_new

    def body(j, carry):
        step(pl.multiple_of(j * ATT_TK, ATT_TK), False)
        return carry

    lax.fori_loop(0, i * (ATT_TQ // ATT_TK), body, 0)
    step(q0, True)
    o0 = acc_scr[0] / l_scr[0]
    o1 = acc_scr[1] / l_scr[1]
    o_ref[0] = jnp.where(first, o0, o1).astype(o_ref.dtype)


def _fox_attn(q, k, v, c_t):
    b, s, t = q.shape
    c4 = c_t[:, :TOK_HEADS, :].reshape(b, HEAD_PAIRS, 2, s)
    return pl.pallas_call(
        _fox_attn_kernel,
        grid=(b, HEAD_PAIRS, s // ATT_TQ),
        in_specs=[
            pl.BlockSpec((1, ATT_TQ, PAIR_WIDTH), lambda bi, p, i: (bi, i, p)),
            pl.BlockSpec((1, s, PAIR_WIDTH), lambda bi, p, i: (bi, 0, p)),
            pl.BlockSpec((1, s, PAIR_WIDTH), lambda bi, p, i: (bi, 0, p)),
            pl.BlockSpec((1, 1, 2, s), lambda bi, p, i: (bi, p, 0, 0)),
        ],
        out_specs=pl.BlockSpec((1, ATT_TQ, PAIR_WIDTH), lambda bi, p, i: (bi, i, p)),
        out_shape=jax.ShapeDtypeStruct((b, s, t), BF16),
        scratch_shapes=[pltpu.VMEM((2, ATT_TQ, PAIR_WIDTH), BF16),
                        pltpu.VMEM((2, ATT_TQ, LANES), F32),
                        pltpu.VMEM((2, ATT_TQ, LANES), F32),
                        pltpu.VMEM((2, ATT_TQ, PAIR_WIDTH), F32)],
        compiler_params=_params(("arbitrary", "arbitrary", "arbitrary")),
        name="fox_attn",
    )(q, k, v, c4)


def _mem_kv_kernel(mem_ref, g_ref, w_ref, gk_ref, ps_ref, pb_ref, k_ref, v_ref):
    mem_n = _rms_rows(mem_ref[0], g_ref[...]).astype(BF16)
    kv = _dot(mem_n, w_ref[...])
    k_ref[0] = _group_rms(kv[:, :MEM_WIDTH], ps_ref[...], pb_ref[...], gk_ref[...]).astype(BF16)
    v_ref[0] = kv[:, MEM_WIDTH:].astype(BF16)


def _mem_kv(mem, g_mem, w_kv, g_k):
    b = mem.shape[0]
    ps, pb = _group_mats(MEM_WIDTH)
    blk = pl.BlockSpec((1, MEM_LEN, MEM_WIDTH), lambda bi: (bi, 0, 0))
    return pl.pallas_call(
        _mem_kv_kernel,
        grid=(b,),
        in_specs=[
            pl.BlockSpec((1, MEM_LEN, D_MODEL), lambda bi: (bi, 0, 0)),
            _const_spec((1, D_MODEL)),
            _const_spec((D_MODEL, 2 * MEM_WIDTH)),
            _const_spec((1, MEM_WIDTH)),
            _const_spec((MEM_WIDTH, LANES)),
            _const_spec((LANES, MEM_WIDTH)),
        ],
        out_specs=[blk, blk],
        out_shape=[jax.ShapeDtypeStruct((b, MEM_LEN, MEM_WIDTH), BF16)] * 2,
        compiler_params=_params(("arbitrary",)),
        name="mem_kv",
    )(mem, g_mem.reshape(1, D_MODEL), w_kv.astype(BF16),
      jnp.tile(g_k, MEM_HEADS).reshape(1, MEM_WIDTH), ps, pb)


def _mix_out_kernel(x_ref, tok_ref, mq_ref, km_ref, vm_ref, wt_ref, wm_ref, o_ref):
    mq = mq_ref[0]
    km = km_ref[0]
    vm = vm_ref[0]
    lane = lax.broadcasted_iota(jnp.int32, (MIX_TM, MEM_WIDTH), 1) // HEAD_DIM
    zero = jnp.zeros_like(mq)
    mo = jnp.zeros((MIX_TM, MEM_WIDTH), F32)
    for hh in range(MEM_HEADS):
        sel = lane == hh
        s = _dot_nt(jnp.where(sel, mq, zero), km)
        p = jnp.exp(s - jnp.max(s, axis=-1, keepdims=True))
        p = p / jnp.sum(p, axis=-1, keepdims=True)
        mo = jnp.where(sel, _dot(p.astype(BF16), vm), mo)
    y = _dot(tok_ref[0], wt_ref[...]) + _dot(mo.astype(BF16), wm_ref[...])
    o_ref[0] = x_ref[0] + y


def _mix_out(x, tok, mq, km, vm, w_out):
    b, s, _ = x.shape
    wt = w_out[:TOK_WIDTH].astype(BF16)
    wm = w_out[TOK_WIDTH:].astype(BF16)
    tile = lambda w: pl.BlockSpec((1, MIX_TM, w), lambda bi, i: (bi, i, 0))
    memblk = pl.BlockSpec((1, MEM_LEN, MEM_WIDTH), lambda bi, i: (bi, 0, 0))
    return pl.pallas_call(
        _mix_out_kernel,
        grid=(b, s // MIX_TM),
        in_specs=[tile(D_MODEL), tile(TOK_WIDTH), tile(MEM_WIDTH), memblk, memblk,
                  _const_spec((TOK_WIDTH, D_MODEL)), _const_spec((MEM_WIDTH, D_MODEL))],
        out_specs=tile(D_MODEL),
        out_shape=jax.ShapeDtypeStruct((b, s, D_MODEL), F32),
        compiler_params=_params(("arbitrary", "arbitrary")),
        name="mix_out",
    )(x, tok, mq, km, vm, wt, wm)


def _gelu_tanh(x):
    return 0.5 * x * (1.0 + jnp.tanh(math.sqrt(2.0 / math.pi) * (x + 0.044715 * (x * x * x))))


def _gmlp_in_kernel(x_ref, g_ref, wuv_ref, wmq_ref, gv_ref, gmq_ref, ws_ref, bs_ref,
                    ps_ref, pb_ref, psm_ref, pbm_ref, tok_ref, mq_ref, vn_scr):
    h = _rms_rows(x_ref[0], g_ref[...]).astype(BF16)
    t = TOK_WIDTH
    n_chunk = MIX_TM // CHUNK
    v = _gelu_tanh(_dot(h, wuv_ref[:, t:2 * t]))
    vn_scr[...] = _group_rms(v, ps_ref[...], pb_ref[...], gv_ref[...]).astype(BF16)
    mq = _dot(h, wmq_ref[...])
    mq_ref[0] = (_group_rms(mq, psm_ref[...], pbm_ref[...], gmq_ref[...]) * QK_SCALE).astype(BF16)

    row = lax.broadcasted_iota(jnp.int32, (CHUNK, CHUNK), 0)
    col = lax.broadcasted_iota(jnp.int32, (CHUNK, CHUNK), 1)
    causal = col <= row
    first = lax.broadcasted_iota(jnp.int32, (CHUNK, PAIR_WIDTH), 1) < HEAD_DIM
    for p in range(HEAD_PAIRS):
        lanes = slice(p * PAIR_WIDTH, (p + 1) * PAIR_WIDTH)
        u = _gelu_tanh(_dot(h, wuv_ref[:, lanes]))
        v_cat = jnp.concatenate(
            [vn_scr[c * CHUNK:(c + 1) * CHUNK, lanes] for c in range(n_chunk)], axis=1)
        wa = jnp.where(causal, ws_ref[2 * p], 0.0).astype(BF16)
        wb = jnp.where(causal, ws_ref[2 * p + 1], 0.0).astype(BF16)
        ga = _dot(wa, v_cat)
        gb = _dot(wb, v_cat)
        bias = bs_ref[:, lanes]
        for c in range(n_chunk):
            cols = slice(c * PAIR_WIDTH, (c + 1) * PAIR_WIDTH)
            gate = jnp.where(first, ga[:, cols], gb[:, cols]) + bias
            rows = slice(c * CHUNK, (c + 1) * CHUNK)
            tok_ref[0, rows, lanes] = (u[rows, :] * gate).astype(BF16)


def _gmlp_in(x, gain, w_in, v_gain, w_s, b_s, g_mq):
    b, s, _ = x.shape
    t = TOK_WIDTH
    wuv = w_in[:, :2 * t].astype(BF16)
    wmq = w_in[:, 2 * t:].astype(BF16)
    bs_full = jnp.repeat(b_s.T, HEAD_DIM, axis=1)
    ps, pb = _group_mats(t)
    psm, pbm = _group_mats(MEM_WIDTH)
    tile = lambda w: pl.BlockSpec((1, MIX_TM, w), lambda bi, i: (bi, i, 0))
    return pl.pallas_call(
        _gmlp_in_kernel,
        grid=(b, s // MIX_TM),
        in_specs=[
            tile(D_MODEL),
            _const_spec((1, D_MODEL)),
            _const_spec((D_MODEL, 2 * t)),
            _const_spec((D_MODEL, MEM_WIDTH)),
            _const_spec((1, t)),
            _const_spec((1, MEM_WIDTH)),
            _const_spec((TOK_HEADS, CHUNK, CHUNK)),
            _const_spec((CHUNK, t)),
            _const_spec((t, LANES)),
            _const_spec((LANES, t)),
            _const_spec((MEM_WIDTH, LANES)),
            _const_spec((LANES, MEM_WIDTH)),
        ],
        out_specs=[tile(t), tile(MEM_WIDTH)],
        out_shape=[jax.ShapeDtypeStruct((b, s, t), BF16),
                   jax.ShapeDtypeStruct((b, s, MEM_WIDTH), BF16)],
        scratch_shapes=[pltpu.VMEM((MIX_TM, t), BF16)],
        compiler_params=_params(("arbitrary", "arbitrary")),
        name="gmlp_in",
    )(x, gain.reshape(1, D_MODEL), wuv, wmq, v_gain.reshape(1, t),
      jnp.tile(g_mq, MEM_HEADS).reshape(1, MEM_WIDTH), w_s, bs_full, ps, pb, psm, pbm)


def kernel(x, mem, norm_ffn1, ffn1_w_in, ffn1_w_out, norm_mix, norm_ffn2, ffn2_w_in, ffn2_w_out,
           w_out, mem_norm, mem_w_kv, mem_q_norm, mem_k_norm, fox_w_in, fox_b_f, fox_q_norm,
           fox_k_norm, gmlp_w_in, gmlp_v_norm, gmlp_w_s, gmlp_b_s):
    b, s, d = x.shape
    depth = norm_mix.shape[0]
    for i in range(depth):
        kind, j = i % 2, i // 2
        x = _ffn(x.reshape(b * s, d), norm_ffn1[i], ffn1_w_in[i], ffn1_w_out[i]).reshape(b, s, d)
        km, vm = _mem_kv(mem, mem_norm, mem_w_kv[i], mem_k_norm[i])
        if kind == 0:
            q, k, v, mq, c_t = _fox_in(x, norm_mix[i], fox_w_in[j], fox_b_f[j], fox_q_norm[j],
                                       fox_k_norm[j], mem_q_norm[i])
            tok = _fox_attn(q, k, v, c_t)
        else:
            tok, mq = _gmlp_in(x, norm_mix[i], gmlp_w_in[j], gmlp_v_norm[j], gmlp_w_s[j],
                               gmlp_b_s[j], mem_q_norm[i])
        x = _mix_out(x, tok, mq, km, vm, w_out[i])
        x = _ffn(x.reshape(b * s, d), norm_ffn2[i], ffn2_w_in[i], ffn2_w_out[i]).reshape(b, s, d)
    return x
```

```python
import math

import jax
import jax.numpy as jnp
from jax import lax
from jax.experimental import pallas as pl
from jax.experimental.pallas import tpu as pltpu

D_MODEL = 1024
D_FF = 2816
HEAD_DIM = 64
MEM_LEN = 256
MEM_HEADS = 4
MEM_WIDTH = MEM_HEADS * HEAD_DIM
TOK_WIDTH = D_MODEL - MEM_WIDTH
TOK_HEADS = TOK_WIDTH // HEAD_DIM
HEAD_PAIRS = TOK_HEADS // 2
CHUNK = 128
EPS = 1e-6

LANES = 128
SUBLANES = 8
PAIR_WIDTH = 2 * HEAD_DIM
NEG_BIG = -1e30
QK_SCALE = 1.0 / math.sqrt(HEAD_DIM)
LOG2E = math.log2(math.e)
SPLIT = 3

FFN_TM = 512
FFN_FC = 256
MIX_TM = 512
ATT_TQ = 512
ATT_TK = ATT_TQ
VMEM_LIMIT = 56 * 1024 * 1024

KEY_BLOCK = 2 * PAIR_WIDTH
ONES_LANE0 = 2 * SPLIT

F32 = jnp.float32
BF16 = jnp.bfloat16


def _dot(a, b):
    return jnp.dot(a, b, preferred_element_type=F32)


def _dot_nt(a, b):
    return lax.dot_general(a, b, (((1,), (1,)), ((), ())), preferred_element_type=F32)


def _split2(y):
    hi = y.astype(BF16)
    lo = (y - hi.astype(F32)).astype(BF16)
    return hi, lo


def _split3(y):
    hi = y.astype(BF16)
    r = y - hi.astype(F32)
    mid = r.astype(BF16)
    lo = (r - mid.astype(F32)).astype(BF16)
    return hi, mid, lo


def _rms_rows(x, gain):
    ms = jnp.mean(x * x, axis=-1, keepdims=True)
    return x * lax.rsqrt(ms + EPS) * gain


def _group_rms(y, p_sum, p_bcast, gain):
    hi, lo = _split2(y * y)
    ss = _dot(hi, p_sum) + _dot(lo, p_sum)
    r = lax.rsqrt(ss * (1.0 / HEAD_DIM) + EPS)
    rh, rl = _split2(r)
    rb = _dot(rh, p_bcast) + _dot(rl, p_bcast)
    return y * rb * gain


def _group_mats(width):
    lane = jnp.arange(width) // HEAD_DIM
    p_sum = (lane[:, None] == jnp.arange(LANES)[None, :]).astype(BF16)
    return p_sum, p_sum.T


def _const_spec(shape):
    nd = len(shape)
    return pl.BlockSpec(shape, lambda *_: (0,) * nd, pipeline_mode=pl.Buffered(1))


def _params(sem):
    return pltpu.CompilerParams(dimension_semantics=sem, vmem_limit_bytes=VMEM_LIMIT)


def _ffn_kernel(x_ref, g_ref, wa_ref, wb_ref, wo_ref, o_ref, h_scr, acc_scr):
    x = x_ref[...]
    h_scr[...] = _rms_rows(x, g_ref[...]).astype(BF16)
    acc_scr[...] = jnp.zeros_like(acc_scr)

    def body(c, carry):
        col = pl.multiple_of(c * FFN_FC, FFN_FC)
        h = h_scr[...]
        a = _dot(h, wa_ref[:, pl.ds(col, FFN_FC)])
        b = _dot(h, wb_ref[:, pl.ds(col, FFN_FC)])
        g = (a * jax.nn.sigmoid(a) * b).astype(BF16)
        acc_scr[...] += _dot(g, wo_ref[pl.ds(col, FFN_FC), :])
        return carry

    lax.fori_loop(0, D_FF // FFN_FC, body, 0)
    o_ref[...] = x + 0.5 * acc_scr[...]


def _ffn(x2, gain, w_in, w_out):
    n = x2.shape[0]
    wa = w_in[:, :D_FF].astype(BF16)
    wb = w_in[:, D_FF:].astype(BF16)
    wo = w_out.astype(BF16)
    return pl.pallas_call(
        _ffn_kernel,
        grid=(n // FFN_TM,),
        in_specs=[
            pl.BlockSpec((FFN_TM, D_MODEL), lambda i: (i, 0)),
            _const_spec((1, D_MODEL)),
            _const_spec((D_MODEL, D_FF)),
            _const_spec((D_MODEL, D_FF)),
            _const_spec((D_FF, D_MODEL)),
        ],
        out_specs=pl.BlockSpec((FFN_TM, D_MODEL), lambda i: (i, 0)),
        out_shape=jax.ShapeDtypeStruct((n, D_MODEL), F32),
        scratch_shapes=[pltpu.VMEM((FFN_TM, D_MODEL), BF16),
                        pltpu.VMEM((FFN_TM, D_MODEL), F32)],
        compiler_params=_params(("arbitrary",)),
        name="ffn",
    )(x2, gain.reshape(1, D_MODEL), wa, wb, wo)


def _fox_in_kernel(x_ref, g_ref, w_ref, bf_ref, gq_ref, gk_ref, gmq_ref,
                   ps_ref, pb_ref, psm_ref, pbm_ref, tri_ref, route_ref, ones_ref,
                   q_ref, k_ref, v_ref, mq_ref, carry_scr):
    @pl.when(pl.program_id(1) == 0)
    def _():
        carry_scr[...] = jnp.zeros_like(carry_scr)

    t = TOK_WIDTH
    h = _rms_rows(x_ref[0], g_ref[...]).astype(BF16)
    y = _dot(h, w_ref[...])
    q = y[:, 0:t]
    q_ref[0] = (_group_rms(q, ps_ref[...], pb_ref[...], gq_ref[...])
                * (QK_SCALE * LOG2E)).astype(BF16)
    kn = _group_rms(y[:, t:2 * t], ps_ref[...], pb_ref[...], gk_ref[...])
    v = y[:, 2 * t:3 * t]
    mq = y[:, 3 * t:3 * t + MEM_WIDTH]
    mq_ref[0] = (_group_rms(mq, psm_ref[...], pbm_ref[...], gmq_ref[...]) * QK_SCALE).astype(BF16)

    f = y[:, 3 * t + MEM_WIDTH:] + bf_ref[...]
    log_f = (jnp.minimum(f, 0.0) - jnp.log1p(jnp.exp(-jnp.abs(f)))) * LOG2E
    tri = tri_ref[...]
    c = sum(_dot(tri, part) for part in _split3(log_f)) + carry_scr[0:1, :]
    carry_scr[...] = jnp.broadcast_to(c[MIX_TM - 1:MIX_TM, :], carry_scr.shape)
    decay = _dot(jnp.concatenate(_split3(c), axis=1), route_ref[...]) + ones_ref[...]

    lane = lax.broadcasted_iota(jnp.int32, (MIX_TM, PAIR_WIDTH), 1)
    first = lane < HEAD_DIM
    ones_col = jnp.where(lane == HEAD_DIM, 1.0, 0.0)
    for p in range(HEAD_PAIRS):
        pair = slice(p * PAIR_WIDTH, (p + 1) * PAIR_WIDTH)
        k_ref[0, :, p * KEY_BLOCK:p * KEY_BLOCK + PAIR_WIDTH] = kn[:, pair].astype(BF16)
        k_ref[0, :, p * KEY_BLOCK + PAIR_WIDTH:(p + 1) * KEY_BLOCK] = decay[:, pair].astype(BF16)
        vp = v[:, pair]
        v_ref[0, :, 2 * p * PAIR_WIDTH:(2 * p + 1) * PAIR_WIDTH] = (
            jnp.where(first, vp, ones_col).astype(BF16))
        v_ref[0, :, (2 * p + 1) * PAIR_WIDTH:(2 * p + 2) * PAIR_WIDTH] = (
            jnp.where(first, pltpu.roll(vp, HEAD_DIM, 1), ones_col).astype(BF16))


def _decay_route():
    r = jnp.zeros((SPLIT * LANES, HEAD_PAIRS * PAIR_WIDTH), F32)
    for hd in range(TOK_HEADS):
        for j in range(SPLIT):
            r = r.at[j * LANES + hd, (hd // 2) * PAIR_WIDTH + (hd % 2) * SPLIT + j].set(-1.0)
    ones = jnp.zeros((1, HEAD_PAIRS * PAIR_WIDTH), F32)
    for p in range(HEAD_PAIRS):
        ones = ones.at[0, p * PAIR_WIDTH + ONES_LANE0:p * PAIR_WIDTH + ONES_LANE0 + SPLIT].set(1.0)
    return r.astype(BF16), ones


def _fox_in(x, gain, w_in, b_f, g_q, g_k, g_mq):
    b, s, _ = x.shape
    t = TOK_WIDTH
    n_gate = 3 * t
    w_all = jnp.concatenate(
        [w_in[:, :n_gate], w_in[:, n_gate + TOK_HEADS:], w_in[:, n_gate:n_gate + TOK_HEADS],
         jnp.zeros((D_MODEL, LANES - TOK_HEADS), w_in.dtype)], axis=1).astype(BF16)
    n_all = w_all.shape[1]
    bf = jnp.zeros((1, LANES), F32).at[0, :TOK_HEADS].set(b_f)
    ps, pb = _group_mats(t)
    psm, pbm = _group_mats(MEM_WIDTH)
    idx = jnp.arange(MIX_TM)
    tri = (idx[None, :] <= idx[:, None]).astype(BF16)
    route, ones = _decay_route()
    tile = lambda w: pl.BlockSpec((1, MIX_TM, w), lambda bi, i: (bi, i, 0))
    return pl.pallas_call(
        _fox_in_kernel,
        grid=(b, s // MIX_TM),
        in_specs=[
            tile(D_MODEL),
            _const_spec((1, D_MODEL)),
            _const_spec((D_MODEL, n_all)),
            _const_spec((1, LANES)),
            _const_spec((1, t)),
            _const_spec((1, t)),
            _const_spec((1, MEM_WIDTH)),
            _const_spec((t, LANES)),
            _const_spec((LANES, t)),
            _const_spec((MEM_WIDTH, LANES)),
            _const_spec((LANES, MEM_WIDTH)),
            _const_spec((MIX_TM, MIX_TM)),
            _const_spec(route.shape),
            _const_spec(ones.shape),
        ],
        out_specs=[tile(t), tile(HEAD_PAIRS * KEY_BLOCK), tile(TOK_HEADS * PAIR_WIDTH),
                   tile(MEM_WIDTH)],
        out_shape=[jax.ShapeDtypeStruct((b, s, t), BF16),
                   jax.ShapeDtypeStruct((b, s, HEAD_PAIRS * KEY_BLOCK), BF16),
                   jax.ShapeDtypeStruct((b, s, TOK_HEADS * PAIR_WIDTH), BF16),
                   jax.ShapeDtypeStruct((b, s, MEM_WIDTH), BF16)],
        scratch_shapes=[pltpu.VMEM((SUBLANES, LANES), F32)],
        compiler_params=_params(("arbitrary", "arbitrary")),
        name="fox_in",
    )(x, gain.reshape(1, D_MODEL), w_all, bf,
      jnp.tile(g_q, TOK_HEADS).reshape(1, t), jnp.tile(g_k, TOK_HEADS).reshape(1, t),
      jnp.tile(g_mq, MEM_HEADS).reshape(1, MEM_WIDTH), ps, pb, psm, pbm, tri, route, ones)


def _fox_attn_kernel(q_ref, k_ref, v_ref, o_ref, qa_scr, sa_scr, sb_scr, m_scr, acc_scr):
    i = pl.program_id(2)
    q0 = pl.multiple_of(i * ATT_TQ, ATT_TQ)
    lane = lax.broadcasted_iota(jnp.int32, (ATT_TQ, PAIR_WIDTH), 1)
    first = lane < HEAD_DIM
    q = q_ref[0]
    zero = jnp.zeros_like(q)
    base = -k_ref[0, pl.ds(q0, 2 * SUBLANES), PAIR_WIDTH:KEY_BLOCK].astype(F32)
    base = jnp.broadcast_to(base[0:1, :], (ATT_TQ, PAIR_WIDTH))
    in_ones = (lane >= ONES_LANE0) & (lane < ONES_LANE0 + SPLIT)
    for hh in range(2):
        own = (lane >= hh * SPLIT) & (lane < (hh + 1) * SPLIT)
        moved = pltpu.roll(base, ONES_LANE0 - hh * SPLIT, 1)
        qa = jnp.where(own, 1.0, jnp.where(in_ones, moved, 0.0))
        qa_scr[hh, :, 0:PAIR_WIDTH] = jnp.where(first == (hh == 0), q, zero)
        qa_scr[hh, :, PAIR_WIDTH:KEY_BLOCK] = qa.astype(BF16)
    m_scr[...] = jnp.full_like(m_scr, NEG_BIG)
    acc_scr[...] = jnp.zeros_like(acc_scr)

    def scores(blk, s_scr):
        kb = k_ref[0, pl.ds(pl.multiple_of(blk * ATT_TK, ATT_TK), ATT_TK), :]
        for hh in range(2):
            s_scr[hh] = _dot_nt(qa_scr[hh], kb)

    def consume(blk, s_scr, masked):
        k0 = pl.multiple_of(blk * ATT_TK, ATT_TK)
        probs = []
        for hh in range(2):
            s = s_scr[hh]
            if masked:
                row = lax.broadcasted_iota(jnp.int32, (ATT_TQ, ATT_TK), 0)
                col = lax.broadcasted_iota(jnp.int32, (ATT_TQ, ATT_TK), 1)
                s = jnp.where(col <= row, s, NEG_BIG)
            m_prev = m_scr[hh]
            m_new = jnp.maximum(m_prev, jnp.max(s, axis=-1, keepdims=True))
            m_scr[hh] = m_new
            p = jnp.concatenate(
                [jnp.exp2(s[:, c * LANES:(c + 1) * LANES] - m_new)
                 for c in range(ATT_TK // LANES)], axis=1).astype(BF16)
            probs.append((p, jnp.exp2(m_prev - m_new)))
        for hh in range(2):
            p, alpha = probs[hh]
            vb = v_ref[0, pl.ds(k0, ATT_TK), hh * PAIR_WIDTH:(hh + 1) * PAIR_WIDTH]
            acc_scr[hh] = alpha * acc_scr[hh] + _dot(p, vb)

    scores(0, sa_scr)

    def body(jj, carry):
        scores(2 * jj + 1, sb_scr)
        consume(2 * jj, sa_scr, False)
        scores(2 * jj + 2, sa_scr)
        consume(2 * jj + 1, sb_scr, False)
        return carry

    lax.fori_loop(0, i // 2, body, 0)

    @pl.when(i % 2 == 0)
    def _():
        consume(i, sa_scr, True)

    @pl.when(i % 2 == 1)
    def _():
        scores(i, sb_scr)
        consume(i - 1, sa_scr, False)
        consume(i, sb_scr, True)

    o0 = acc_scr[0]
    o1 = acc_scr[1]
    o0 = o0 / o0[:, HEAD_DIM:HEAD_DIM + 1]
    o1 = o1 / o1[:, HEAD_DIM:HEAD_DIM + 1]
    o_ref[0] = jnp.where(first, o0, pltpu.roll(o1, HEAD_DIM, 1)).astype(o_ref.dtype)


def _fox_attn(q, k, v):
    b, s, t = q.shape
    return pl.pallas_call(
        _fox_attn_kernel,
        grid=(b, HEAD_PAIRS, s // ATT_TQ),
        in_specs=[
            pl.BlockSpec((1, ATT_TQ, PAIR_WIDTH), lambda bi, p, i: (bi, i, p)),
            pl.BlockSpec((1, s, KEY_BLOCK), lambda bi, p, i: (bi, 0, p)),
            pl.BlockSpec((1, s, 2 * PAIR_WIDTH), lambda bi, p, i: (bi, 0, p)),
        ],
        out_specs=pl.BlockSpec((1, ATT_TQ, PAIR_WIDTH), lambda bi, p, i: (bi, i, p)),
        out_shape=jax.ShapeDtypeStruct((b, s, t), BF16),
        scratch_shapes=[pltpu.VMEM((2, ATT_TQ, KEY_BLOCK), BF16),
                        pltpu.VMEM((2, ATT_TQ, ATT_TK), F32),
                        pltpu.VMEM((2, ATT_TQ, ATT_TK), F32),
                        pltpu.VMEM((2, ATT_TQ, LANES), F32),
                        pltpu.VMEM((2, ATT_TQ, PAIR_WIDTH), F32)],
        compiler_params=_params(("arbitrary", "arbitrary", "arbitrary")),
        name="fox_attn",
    )(q, k, v)


def _mem_kv_kernel(mem_ref, g_ref, w_ref, gk_ref, ps_ref, pb_ref, k_ref, v_ref):
    mem_n = _rms_rows(mem_ref[0], g_ref[...]).astype(BF16)
    kv = _dot(mem_n, w_ref[...])
    k_ref[0] = _group_rms(kv[:, :MEM_WIDTH], ps_ref[...], pb_ref[...], gk_ref[...]).astype(BF16)
    v_ref[0] = kv[:, MEM_WIDTH:].astype(BF16)


def _mem_kv(mem, g_mem, w_kv, g_k):
    b = mem.shape[0]
    ps, pb = _group_mats(MEM_WIDTH)
    blk = pl.BlockSpec((1, MEM_LEN, MEM_WIDTH), lambda bi: (bi, 0, 0))
    return pl.pallas_call(
        _mem_kv_kernel,
        grid=(b,),
        in_specs=[
            pl.BlockSpec((1, MEM_LEN, D_MODEL), lambda bi: (bi, 0, 0)),
            _const_spec((1, D_MODEL)),
            _const_spec((D_MODEL, 2 * MEM_WIDTH)),
            _const_spec((1, MEM_WIDTH)),
            _const_spec((MEM_WIDTH, LANES)),
            _const_spec((LANES, MEM_WIDTH)),
        ],
        out_specs=[blk, blk],
        out_shape=[jax.ShapeDtypeStruct((b, MEM_LEN, MEM_WIDTH), BF16)] * 2,
        compiler_params=_params(("arbitrary",)),
        name="mem_kv",
    )(mem, g_mem.reshape(1, D_MODEL), w_kv.astype(BF16),
      jnp.tile(g_k, MEM_HEADS).reshape(1, MEM_WIDTH), ps, pb)


def _mix_out_kernel(x_ref, tok_ref, mq_ref, km_ref, vm_ref, wt_ref, wm_ref, o_ref):
    mq = mq_ref[0]
    km = km_ref[0]
    vm = vm_ref[0]
    lane = lax.broadcasted_iota(jnp.int32, (MIX_TM, MEM_WIDTH), 1) // HEAD_DIM
    zero = jnp.zeros_like(mq)
    mo = jnp.zeros((MIX_TM, MEM_WIDTH), F32)
    for hh in range(MEM_HEADS):
        sel = lane == hh
        s = _dot_nt(jnp.where(sel, mq, zero), km)
        p = jnp.exp(s - jnp.max(s, axis=-1, keepdims=True))
        p = p / jnp.sum(p, axis=-1, keepdims=True)
        mo = jnp.where(sel, _dot(p.astype(BF16), vm), mo)
    y = _dot(tok_ref[0], wt_ref[...]) + _dot(mo.astype(BF16), wm_ref[...])
    o_ref[0] = x_ref[0] + y


def _mix_out(x, tok, mq, km, vm, w_out):
    b, s, _ = x.shape
    wt = w_out[:TOK_WIDTH].astype(BF16)
    wm = w_out[TOK_WIDTH:].astype(BF16)
    tile = lambda w: pl.BlockSpec((1, MIX_TM, w), lambda bi, i: (bi, i, 0))
    memblk = pl.BlockSpec((1, MEM_LEN, MEM_WIDTH), lambda bi, i: (bi, 0, 0))
    return pl.pallas_call(
        _mix_out_kernel,
        grid=(b, s // MIX_TM),
        in_specs=[tile(D_MODEL), tile(TOK_WIDTH), tile(MEM_WIDTH), memblk, memblk,
                  _const_spec((TOK_WIDTH, D_MODEL)), _const_spec((MEM_WIDTH, D_MODEL))],
        out_specs=tile(D_MODEL),
        out_shape=jax.ShapeDtypeStruct((b, s, D_MODEL), F32),
        compiler_params=_params(("arbitrary", "arbitrary")),
        name="mix_out",
    )(x, tok, mq, km, vm, wt, wm)


def _gelu_tanh(x):
    return 0.5 * x * (1.0 + jnp.tanh(math.sqrt(2.0 / math.pi) * (x + 0.044715 * (x * x * x))))


def _gmlp_in_kernel(x_ref, g_ref, wuv_ref, wmq_ref, gv_ref, gmq_ref, ws_ref, bs_ref,
                    ps_ref, pb_ref, psm_ref, pbm_ref, tok_ref, mq_ref, vn_scr):
    h = _rms_rows(x_ref[0], g_ref[...]).astype(BF16)
    t = TOK_WIDTH
    n_chunk = MIX_TM // CHUNK
    v = _gelu_tanh(_dot(h, wuv_ref[:, t:2 * t]))
    vn_scr[...] = _group_rms(v, ps_ref[...], pb_ref[...], gv_ref[...]).astype(BF16)
    mq = _dot(h, wmq_ref[...])
    mq_ref[0] = (_group_rms(mq, psm_ref[...], pbm_ref[...], gmq_ref[...]) * QK_SCALE).astype(BF16)

    row = lax.broadcasted_iota(jnp.int32, (CHUNK, CHUNK), 0)
    col = lax.broadcasted_iota(jnp.int32, (CHUNK, CHUNK), 1)
    causal = col <= row
    first = lax.broadcasted_iota(jnp.int32, (CHUNK, PAIR_WIDTH), 1) < HEAD_DIM
    for p in range(HEAD_PAIRS):
        lanes = slice(p * PAIR_WIDTH, (p + 1) * PAIR_WIDTH)
        u = _gelu_tanh(_dot(h, wuv_ref[:, lanes]))
        v_cat = jnp.concatenate(
            [vn_scr[c * CHUNK:(c + 1) * CHUNK, lanes] for c in range(n_chunk)], axis=1)
        wa = jnp.where(causal, ws_ref[2 * p], 0.0).astype(BF16)
        wb = jnp.where(causal, ws_ref[2 * p + 1], 0.0).astype(BF16)
        ga = _dot(wa, v_cat)
        gb = _dot(wb, v_cat)
        bias = bs_ref[:, lanes]
        for c in range(n_chunk):
            cols = slice(c * PAIR_WIDTH, (c + 1) * PAIR_WIDTH)
            gate = jnp.where(first, ga[:, cols], gb[:, cols]) + bias
            rows = slice(c * CHUNK, (c + 1) * CHUNK)
            tok_ref[0, rows, lanes] = (u[rows, :] * gate).astype(BF16)


def _gmlp_in(x, gain, w_in, v_gain, w_s, b_s, g_mq):
    b, s, _ = x.shape
    t = TOK_WIDTH
    wuv = w_in[:, :2 * t].astype(BF16)
    wmq = w_in[:, 2 * t:].astype(BF16)
    bs_full = jnp.repeat(b_s.T, HEAD_DIM, axis=1)
    ps, pb = _group_mats(t)
    psm, pbm = _group_mats(MEM_WIDTH)
    tile = lambda w: pl.BlockSpec((1, MIX_TM, w), lambda bi, i: (bi, i, 0))
    return pl.pallas_call(
        _gmlp_in_kernel,
        grid=(b, s // MIX_TM),
        in_specs=[
            tile(D_MODEL),
            _const_spec((1, D_MODEL)),
            _const_spec((D_MODEL, 2 * t)),
            _const_spec((D_MODEL, MEM_WIDTH)),
            _const_spec((1, t)),
            _const_spec((1, MEM_WIDTH)),
            _const_spec((TOK_HEADS, CHUNK, CHUNK)),
            _const_spec((CHUNK, t)),
            _const_spec((t, LANES)),
            _const_spec((LANES, t)),
            _const_spec((MEM_WIDTH, LANES)),
            _const_spec((LANES, MEM_WIDTH)),
        ],
        out_specs=[tile(t), tile(MEM_WIDTH)],
        out_shape=[jax.ShapeDtypeStruct((b, s, t), BF16),
                   jax.ShapeDtypeStruct((b, s, MEM_WIDTH), BF16)],
        scratch_shapes=[pltpu.VMEM((MIX_TM, t), BF16)],
        compiler_params=_params(("arbitrary", "arbitrary")),
        name="gmlp_in",
    )(x, gain.reshape(1, D_MODEL), wuv, wmq, v_gain.reshape(1, t),
      jnp.tile(g_mq, MEM_HEADS).reshape(1, MEM_WIDTH), w_s, bs_full, ps, pb, psm, pbm)


def kernel(x, mem, norm_ffn1, ffn1_w_in, ffn1_w_out, norm_mix, norm_ffn2, ffn2_w_in, ffn2_w_out,
           w_out, mem_norm, mem_w_kv, mem_q_norm, mem_k_norm, fox_w_in, fox_b_f, fox_q_norm,
           fox_k_norm, gmlp_w_in, gmlp_v_norm, gmlp_w_s, gmlp_b_s):
    b, s, d = x.shape
    depth = norm_mix.shape[0]
    for i in range(depth):
        kind, j = i % 2, i // 2
        x = _ffn(x.reshape(b * s, d), norm_ffn1[i], ffn1_w_in[i], ffn1_w_out[i]).reshape(b, s, d)
        km, vm = _mem_kv(mem, mem_norm, mem_w_kv[i], mem_k_norm[i])
        if kind == 0:
            q, k, v, mq = _fox_in(x, norm_mix[i], fox_w_in[j], fox_b_f[j], fox_q_norm[j],
                                  fox_k_norm[j], mem_q_norm[i])
            tok = _fox_attn(q, k, v)
        else:
            tok, mq = _gmlp_in(x, norm_mix[i], gmlp_w_in[j], gmlp_v_norm[j], gmlp_w_s[j],
                               gmlp_b_s[j], mem_q_norm[i])
        x = _mix_out(x, tok, mq, km, vm, w_out[i])
        x = _ffn(x.reshape(b * s, d), norm_ffn2[i], ffn2_w_in[i], ffn2_w_out[i]).reshape(b, s, d)
    return x
```

```python
import math

import jax
import jax.numpy as jnp
from jax import lax
from jax.experimental import pallas as pl
from jax.experimental.pallas import tpu as pltpu

D_MODEL = 1024
D_FF = 2816
HEAD_DIM = 64
MEM_LEN = 256
MEM_HEADS = 4
MEM_WIDTH = MEM_HEADS * HEAD_DIM
TOK_WIDTH = D_MODEL - MEM_WIDTH
TOK_HEADS = TOK_WIDTH // HEAD_DIM
HEAD_PAIRS = TOK_HEADS // 2
CHUNK = 128
EPS = 1e-6

LANES = 128
MXU_WIDTH = 256
SUBLANES = 8
PAIR_WIDTH = 2 * HEAD_DIM
NEG_BIG = -1e30
QK_SCALE = 1.0 / math.sqrt(HEAD_DIM)
LOG2E = math.log2(math.e)
SPLIT = 3

FFN_TM = 512
FFN_FC = 256
MIX_TM = 512
ATT_TQ = 512
ATT_TK = ATT_TQ
VMEM_LIMIT = 56 * 1024 * 1024

KEY_BLOCK = 2 * PAIR_WIDTH
ONES_LANE0 = 2 * SPLIT

F32 = jnp.float32
BF16 = jnp.bfloat16


def _dot(a, b):
    return jnp.dot(a, b, preferred_element_type=F32)


def _dot_nt(a, b):
    return lax.dot_general(a, b, (((1,), (1,)), ((), ())), preferred_element_type=F32)


def _split2(y):
    hi = y.astype(BF16)
    lo = (y - hi.astype(F32)).astype(BF16)
    return hi, lo


def _split3(y):
    hi = y.astype(BF16)
    r = y - hi.astype(F32)
    mid = r.astype(BF16)
    lo = (r - mid.astype(F32)).astype(BF16)
    return hi, mid, lo


def _rms_rows(x, gain):
    ms = jnp.mean(x * x, axis=-1, keepdims=True)
    return x * lax.rsqrt(ms + EPS) * gain


def _group_rms(y, p_blk, gain):
    hi, lo = _split2(y * y)
    ss = jnp.concatenate(
        [_dot(hi[:, c:c + MXU_WIDTH], p_blk) + _dot(lo[:, c:c + MXU_WIDTH], p_blk)
         for c in range(0, y.shape[1], MXU_WIDTH)], axis=1)
    return y * lax.rsqrt(ss * (1.0 / HEAD_DIM) + EPS) * gain


def _group_block():
    grp = jnp.arange(MXU_WIDTH) // HEAD_DIM
    return (grp[:, None] == grp[None, :]).astype(BF16)


def _const_spec(shape):
    nd = len(shape)
    return pl.BlockSpec(shape, lambda *_: (0,) * nd, pipeline_mode=pl.Buffered(1))


def _params(sem):
    return pltpu.CompilerParams(dimension_semantics=sem, vmem_limit_bytes=VMEM_LIMIT)


def _ffn_kernel(x_ref, g_ref, wa_ref, wb_ref, wo_ref, o_ref, h_scr, g_scr):
    h_scr[...] = _rms_rows(x_ref[...], g_ref[...]).astype(BF16)
    for c in range(D_FF // FFN_FC):
        cols = slice(c * FFN_FC, (c + 1) * FFN_FC)
        h = h_scr[...]
        a = _dot(h, wa_ref[:, cols])
        b = _dot(h, wb_ref[:, cols])
        g_scr[:, cols] = (a * jax.nn.sigmoid(a) * b).astype(BF16)
    o_ref[...] = x_ref[...] + 0.5 * _dot(g_scr[...], wo_ref[...])


def _ffn(x2, gain, w_in, w_out):
    n = x2.shape[0]
    wa = w_in[:, :D_FF].astype(BF16)
    wb = w_in[:, D_FF:].astype(BF16)
    wo = w_out.astype(BF16)
    return pl.pallas_call(
        _ffn_kernel,
        grid=(n // FFN_TM,),
        in_specs=[
            pl.BlockSpec((FFN_TM, D_MODEL), lambda i: (i, 0)),
            _const_spec((1, D_MODEL)),
            _const_spec((D_MODEL, D_FF)),
            _const_spec((D_MODEL, D_FF)),
            _const_spec((D_FF, D_MODEL)),
        ],
        out_specs=pl.BlockSpec((FFN_TM, D_MODEL), lambda i: (i, 0)),
        out_shape=jax.ShapeDtypeStruct((n, D_MODEL), F32),
        scratch_shapes=[pltpu.VMEM((FFN_TM, D_MODEL), BF16),
                        pltpu.VMEM((FFN_TM, D_FF), BF16)],
        compiler_params=_params(("arbitrary",)),
        name="ffn",
    )(x2, gain.reshape(1, D_MODEL), wa, wb, wo)


def _fox_in_kernel(x_ref, g_ref, w_ref, bf_ref, gq_ref, gk_ref, gmq_ref,
                   gb_ref, tri_ref, route_ref, ones_ref,
                   q_ref, k_ref, v_ref, mq_ref, carry_scr):
    @pl.when(pl.program_id(1) == 0)
    def _():
        carry_scr[...] = jnp.zeros_like(carry_scr)

    t = TOK_WIDTH
    h = _rms_rows(x_ref[0], g_ref[...]).astype(BF16)
    f = _dot(h, w_ref[:, 3 * t + MEM_WIDTH:]) + bf_ref[...]
    q = _dot(h, w_ref[:, 0:t])
    log_f = (jnp.minimum(f, 0.0) - jnp.log1p(jnp.exp(-jnp.abs(f)))) * LOG2E
    k = _dot(h, w_ref[:, t:2 * t])
    tri = tri_ref[...]
    c = sum(_dot(tri, part) for part in _split3(log_f)) + carry_scr[0:1, :]
    carry_scr[...] = jnp.broadcast_to(c[MIX_TM - 1:MIX_TM, :], carry_scr.shape)
    v = _dot(h, w_ref[:, 2 * t:3 * t])
    mq = _dot(h, w_ref[:, 3 * t:3 * t + MEM_WIDTH])
    q_ref[0] = (_group_rms(q, gb_ref[...], gq_ref[...])
                * (QK_SCALE * LOG2E)).astype(BF16)
    decay = _dot(jnp.concatenate(_split3(c), axis=1), route_ref[...]) + ones_ref[...]
    kn = _group_rms(k, gb_ref[...], gk_ref[...])
    mq_ref[0] = (_group_rms(mq, gb_ref[...], gmq_ref[...]) * QK_SCALE).astype(BF16)

    lane = lax.broadcasted_iota(jnp.int32, (MIX_TM, PAIR_WIDTH), 1)
    first = lane < HEAD_DIM
    ones_col = jnp.where(lane == HEAD_DIM, 1.0, 0.0)
    for p in range(HEAD_PAIRS):
        pair = slice(p * PAIR_WIDTH, (p + 1) * PAIR_WIDTH)
        k_ref[0, :, p * KEY_BLOCK:p * KEY_BLOCK + PAIR_WIDTH] = kn[:, pair].astype(BF16)
        k_ref[0, :, p * KEY_BLOCK + PAIR_WIDTH:(p + 1) * KEY_BLOCK] = decay[:, pair].astype(BF16)
        vp = v[:, pair]
        v_ref[0, :, 2 * p * PAIR_WIDTH:(2 * p + 1) * PAIR_WIDTH] = (
            jnp.where(first, vp, ones_col).astype(BF16))
        v_ref[0, :, (2 * p + 1) * PAIR_WIDTH:(2 * p + 2) * PAIR_WIDTH] = (
            jnp.where(first, pltpu.roll(vp, HEAD_DIM, 1), ones_col).astype(BF16))


def _decay_route():
    r = jnp.zeros((SPLIT * LANES, HEAD_PAIRS * PAIR_WIDTH), F32)
    for hd in range(TOK_HEADS):
        for j in range(SPLIT):
            r = r.at[j * LANES + hd, (hd // 2) * PAIR_WIDTH + (hd % 2) * SPLIT + j].set(-1.0)
    ones = jnp.zeros((1, HEAD_PAIRS * PAIR_WIDTH), F32)
    for p in range(HEAD_PAIRS):
        ones = ones.at[0, p * PAIR_WIDTH + ONES_LANE0:p * PAIR_WIDTH + ONES_LANE0 + SPLIT].set(1.0)
    return r.astype(BF16), ones


def _fox_in(x, gain, w_in, b_f, g_q, g_k, g_mq):
    b, s, _ = x.shape
    t = TOK_WIDTH
    n_gate = 3 * t
    w_all = jnp.concatenate(
        [w_in[:, :n_gate], w_in[:, n_gate + TOK_HEADS:], w_in[:, n_gate:n_gate + TOK_HEADS],
         jnp.zeros((D_MODEL, LANES - TOK_HEADS), w_in.dtype)], axis=1).astype(BF16)
    n_all = w_all.shape[1]
    bf = jnp.zeros((1, LANES), F32).at[0, :TOK_HEADS].set(b_f)
    idx = jnp.arange(MIX_TM)
    tri = (idx[None, :] <= idx[:, None]).astype(BF16)
    route, ones = _decay_route()
    tile = lambda w: pl.BlockSpec((1, MIX_TM, w), lambda bi, i: (bi, i, 0))
    return pl.pallas_call(
        _fox_in_kernel,
        grid=(b, s // MIX_TM),
        in_specs=[
            tile(D_MODEL),
            _const_spec((1, D_MODEL)),
            _const_spec((D_MODEL, n_all)),
            _const_spec((1, LANES)),
            _const_spec((1, t)),
            _const_spec((1, t)),
            _const_spec((1, MEM_WIDTH)),
            _const_spec((MXU_WIDTH, MXU_WIDTH)),
            _const_spec((MIX_TM, MIX_TM)),
            _const_spec(route.shape),
            _const_spec(ones.shape),
        ],
        out_specs=[tile(t), tile(HEAD_PAIRS * KEY_BLOCK), tile(TOK_HEADS * PAIR_WIDTH),
                   tile(MEM_WIDTH)],
        out_shape=[jax.ShapeDtypeStruct((b, s, t), BF16),
                   jax.ShapeDtypeStruct((b, s, HEAD_PAIRS * KEY_BLOCK), BF16),
                   jax.ShapeDtypeStruct((b, s, TOK_HEADS * PAIR_WIDTH), BF16),
                   jax.ShapeDtypeStruct((b, s, MEM_WIDTH), BF16)],
        scratch_shapes=[pltpu.VMEM((SUBLANES, LANES), F32)],
        compiler_params=_params(("arbitrary", "arbitrary")),
        name="fox_in",
    )(x, gain.reshape(1, D_MODEL), w_all, bf,
      jnp.tile(g_q, TOK_HEADS).reshape(1, t), jnp.tile(g_k, TOK_HEADS).reshape(1, t),
      jnp.tile(g_mq, MEM_HEADS).reshape(1, MEM_WIDTH), _group_block(), tri, route, ones)


def _fox_attn_kernel(q_ref, k_ref, v_ref, o_ref, qa_scr, sa_scr, sb_scr, m_scr, acc_scr):
    i = pl.program_id(2)
    q0 = pl.multiple_of(i * ATT_TQ, ATT_TQ)
    lane = lax.broadcasted_iota(jnp.int32, (ATT_TQ, PAIR_WIDTH), 1)
    first = lane < HEAD_DIM
    q = q_ref[0]
    zero = jnp.zeros_like(q)
    base = -k_ref[0, pl.ds(q0, 2 * SUBLANES), PAIR_WIDTH:KEY_BLOCK].astype(F32)
    base = jnp.broadcast_to(base[0:1, :], (ATT_TQ, PAIR_WIDTH))
    in_ones = (lane >= ONES_LANE0) & (lane < ONES_LANE0 + SPLIT)
    for hh in range(2):
        own = (lane >= hh * SPLIT) & (lane < (hh + 1) * SPLIT)
        moved = pltpu.roll(base, ONES_LANE0 - hh * SPLIT, 1)
        qa = jnp.where(own, 1.0, jnp.where(in_ones, moved, 0.0))
        qa_scr[hh, :, 0:PAIR_WIDTH] = jnp.where(first == (hh == 0), q, zero)
        qa_scr[hh, :, PAIR_WIDTH:KEY_BLOCK] = qa.astype(BF16)
    m_scr[...] = jnp.full_like(m_scr, NEG_BIG)
    acc_scr[...] = jnp.zeros_like(acc_scr)

    def scores(blk, s_scr):
        kb = k_ref[0, pl.ds(pl.multiple_of(blk * ATT_TK, ATT_TK), ATT_TK), :]
        for hh in range(2):
            s_scr[hh] = _dot_nt(qa_scr[hh], kb)

    def consume(blk, s_scr, masked):
        k0 = pl.multiple_of(blk * ATT_TK, ATT_TK)
        probs = []
        for hh in range(2):
            s = s_scr[hh]
            if masked:
                row = lax.broadcasted_iota(jnp.int32, (ATT_TQ, ATT_TK), 0)
                col = lax.broadcasted_iota(jnp.int32, (ATT_TQ, ATT_TK), 1)
                s = jnp.where(col <= row, s, NEG_BIG)
            m_prev = m_scr[hh]
            m_new = jnp.maximum(m_prev, jnp.max(s, axis=-1, keepdims=True))
            m_scr[hh] = m_new
            p = jnp.concatenate(
                [jnp.exp2(s[:, c * LANES:(c + 1) * LANES] - m_new)
                 for c in range(ATT_TK // LANES)], axis=1).astype(BF16)
            probs.append((p, jnp.exp2(m_prev - m_new)))
        for hh in range(2):
            p, alpha = probs[hh]
            vb = v_ref[0, pl.ds(k0, ATT_TK), hh * PAIR_WIDTH:(hh + 1) * PAIR_WIDTH]
            acc_scr[hh] = alpha * acc_scr[hh] + _dot(p, vb)

    scores(0, sa_scr)

    def body(jj, carry):
        scores(2 * jj + 1, sb_scr)
        consume(2 * jj, sa_scr, False)
        scores(2 * jj + 2, sa_scr)
        consume(2 * jj + 1, sb_scr, False)
        return carry

    lax.fori_loop(0, i // 2, body, 0)

    @pl.when(i % 2 == 0)
    def _():
        consume(i, sa_scr, True)

    @pl.when(i % 2 == 1)
    def _():
        scores(i, sb_scr)
        consume(i - 1, sa_scr, False)
        consume(i, sb_scr, True)

    o0 = acc_scr[0]
    o1 = acc_scr[1]
    o0 = o0 / o0[:, HEAD_DIM:HEAD_DIM + 1]
    o1 = o1 / o1[:, HEAD_DIM:HEAD_DIM + 1]
    o_ref[0] = jnp.where(first, o0, pltpu.roll(o1, HEAD_DIM, 1)).astype(o_ref.dtype)


def _fox_attn(q, k, v):
    b, s, t = q.shape
    return pl.pallas_call(
        _fox_attn_kernel,
        grid=(b, HEAD_PAIRS, s // ATT_TQ),
        in_specs=[
            pl.BlockSpec((1, ATT_TQ, PAIR_WIDTH), lambda bi, p, i: (bi, i, p)),
            pl.BlockSpec((1, s, KEY_BLOCK), lambda bi, p, i: (bi, 0, p)),
            pl.BlockSpec((1, s, 2 * PAIR_WIDTH), lambda bi, p, i: (bi, 0, p)),
        ],
        out_specs=pl.BlockSpec((1, ATT_TQ, PAIR_WIDTH), lambda bi, p, i: (bi, i, p)),
        out_shape=jax.ShapeDtypeStruct((b, s, t), BF16),
        scratch_shapes=[pltpu.VMEM((2, ATT_TQ, KEY_BLOCK), BF16),
                        pltpu.VMEM((2, ATT_TQ, ATT_TK), F32),
                        pltpu.VMEM((2, ATT_TQ, ATT_TK), F32),
                        pltpu.VMEM((2, ATT_TQ, LANES), F32),
                        pltpu.VMEM((2, ATT_TQ, PAIR_WIDTH), F32)],
        compiler_params=_params(("arbitrary", "arbitrary", "arbitrary")),
        name="fox_attn",
    )(q, k, v)


def _mem_kv_kernel(mem_ref, g_ref, w_ref, gk_ref, gb_ref, k_ref, v_ref):
    mem_n = _rms_rows(mem_ref[0], g_ref[...]).astype(BF16)
    kv = _dot(mem_n, w_ref[...])
    k_ref[0] = _group_rms(kv[:, :MEM_WIDTH], gb_ref[...], gk_ref[...]).astype(BF16)
    v_ref[0] = kv[:, MEM_WIDTH:].astype(BF16)


def _mem_kv(mem, g_mem, w_kv, g_k):
    b = mem.shape[0]
    blk = pl.BlockSpec((1, MEM_LEN, MEM_WIDTH), lambda bi: (bi, 0, 0))
    return pl.pallas_call(
        _mem_kv_kernel,
        grid=(b,),
        in_specs=[
            pl.BlockSpec((1, MEM_LEN, D_MODEL), lambda bi: (bi, 0, 0)),
            _const_spec((1, D_MODEL)),
            _const_spec((D_MODEL, 2 * MEM_WIDTH)),
            _const_spec((1, MEM_WIDTH)),
            _const_spec((MXU_WIDTH, MXU_WIDTH)),
        ],
        out_specs=[blk, blk],
        out_shape=[jax.ShapeDtypeStruct((b, MEM_LEN, MEM_WIDTH), BF16)] * 2,
        compiler_params=_params(("arbitrary",)),
        name="mem_kv",
    )(mem, g_mem.reshape(1, D_MODEL), w_kv.astype(BF16),
      jnp.tile(g_k, MEM_HEADS).reshape(1, MEM_WIDTH), _group_block())


def _mix_out_kernel(x_ref, tok_ref, mq_ref, km_ref, vm_ref, wt_ref, wm_ref, o_ref):
    mq = mq_ref[0]
    km = km_ref[0]
    vm = vm_ref[0]
    lane = lax.broadcasted_iota(jnp.int32, (MIX_TM, MEM_WIDTH), 1) // HEAD_DIM
    zero = jnp.zeros_like(mq)
    scores = [_dot_nt(jnp.where(lane == hh, mq, zero), km) for hh in range(MEM_HEADS)]
    y = _dot(tok_ref[0], wt_ref[...])
    probs = []
    for s in scores:
        p = jnp.exp(s - jnp.max(s, axis=-1, keepdims=True))
        probs.append((p / jnp.sum(p, axis=-1, keepdims=True)).astype(BF16))
    mo = jnp.zeros((MIX_TM, MEM_WIDTH), F32)
    for hh in range(MEM_HEADS):
        mo = jnp.where(lane == hh, _dot(probs[hh], vm), mo)
    o_ref[0] = x_ref[0] + y + _dot(mo.astype(BF16), wm_ref[...])


def _mix_out(x, tok, mq, km, vm, w_out):
    b, s, _ = x.shape
    wt = w_out[:TOK_WIDTH].astype(BF16)
    wm = w_out[TOK_WIDTH:].astype(BF16)
    tile = lambda w: pl.BlockSpec((1, MIX_TM, w), lambda bi, i: (bi, i, 0))
    memblk = pl.BlockSpec((1, MEM_LEN, MEM_WIDTH), lambda bi, i: (bi, 0, 0))
    return pl.pallas_call(
        _mix_out_kernel,
        grid=(b, s // MIX_TM),
        in_specs=[tile(D_MODEL), tile(TOK_WIDTH), tile(MEM_WIDTH), memblk, memblk,
                  _const_spec((TOK_WIDTH, D_MODEL)), _const_spec((MEM_WIDTH, D_MODEL))],
        out_specs=tile(D_MODEL),
        out_shape=jax.ShapeDtypeStruct((b, s, D_MODEL), F32),
        compiler_params=_params(("arbitrary", "arbitrary")),
        name="mix_out",
    )(x, tok, mq, km, vm, wt, wm)


def _gelu_tanh(x):
    return 0.5 * x * (1.0 + jnp.tanh(math.sqrt(2.0 / math.pi) * (x + 0.044715 * (x * x * x))))


def _gmlp_in_kernel(x_ref, g_ref, wuv_ref, wmq_ref, gv_ref, gmq_ref, ws_ref, bs_ref,
                    gb_ref, tok_ref, mq_ref, vn_scr, u_scr):
    h = _rms_rows(x_ref[0], g_ref[...]).astype(BF16)
    t = TOK_WIDTH
    n_chunk = MIX_TM // CHUNK
    v = _dot(h, wuv_ref[:, t:2 * t])
    mq = _dot(h, wmq_ref[...])
    u_scr[...] = _gelu_tanh(_dot(h, wuv_ref[:, 0:t]))
    vn_scr[...] = _group_rms(_gelu_tanh(v), gb_ref[...], gv_ref[...]).astype(BF16)
    mq_ref[0] = (_group_rms(mq, gb_ref[...], gmq_ref[...]) * QK_SCALE).astype(BF16)

    row = lax.broadcasted_iota(jnp.int32, (CHUNK, CHUNK), 0)
    col = lax.broadcasted_iota(jnp.int32, (CHUNK, CHUNK), 1)
    causal = col <= row
    first = lax.broadcasted_iota(jnp.int32, (CHUNK, PAIR_WIDTH), 1) < HEAD_DIM
    for p in range(HEAD_PAIRS):
        lanes = slice(p * PAIR_WIDTH, (p + 1) * PAIR_WIDTH)
        v_cat = jnp.concatenate(
            [vn_scr[c * CHUNK:(c + 1) * CHUNK, lanes] for c in range(n_chunk)], axis=1)
        wa = jnp.where(causal, ws_ref[2 * p], 0.0).astype(BF16)
        wb = jnp.where(causal, ws_ref[2 * p + 1], 0.0).astype(BF16)
        ga = _dot(wa, v_cat)
        gb = _dot(wb, v_cat)
        bias = bs_ref[:, lanes]
        for c in range(n_chunk):
            cols = slice(c * PAIR_WIDTH, (c + 1) * PAIR_WIDTH)
            gate = jnp.where(first, ga[:, cols], gb[:, cols]) + bias
            rows = slice(c * CHUNK, (c + 1) * CHUNK)
            tok_ref[0, rows, lanes] = (u_scr[rows, lanes] * gate).astype(BF16)


def _gmlp_in(x, gain, w_in, v_gain, w_s, b_s, g_mq):
    b, s, _ = x.shape
    t = TOK_WIDTH
    wuv = w_in[:, :2 * t].astype(BF16)
    wmq = w_in[:, 2 * t:].astype(BF16)
    bs_full = jnp.repeat(b_s.T, HEAD_DIM, axis=1)
    tile = lambda w: pl.BlockSpec((1, MIX_TM, w), lambda bi, i: (bi, i, 0))
    return pl.pallas_call(
        _gmlp_in_kernel,
        grid=(b, s // MIX_TM),
        in_specs=[
            tile(D_MODEL),
            _const_spec((1, D_MODEL)),
            _const_spec((D_MODEL, 2 * t)),
            _const_spec((D_MODEL, MEM_WIDTH)),
            _const_spec((1, t)),
            _const_spec((1, MEM_WIDTH)),
            _const_spec((TOK_HEADS, CHUNK, CHUNK)),
            _const_spec((CHUNK, t)),
            _const_spec((MXU_WIDTH, MXU_WIDTH)),
        ],
        out_specs=[tile(t), tile(MEM_WIDTH)],
        out_shape=[jax.ShapeDtypeStruct((b, s, t), BF16),
                   jax.ShapeDtypeStruct((b, s, MEM_WIDTH), BF16)],
        scratch_shapes=[pltpu.VMEM((MIX_TM, t), BF16), pltpu.VMEM((MIX_TM, t), F32)],
        compiler_params=_params(("arbitrary", "arbitrary")),
        name="gmlp_in",
    )(x, gain.reshape(1, D_MODEL), wuv, wmq, v_gain.reshape(1, t),
      jnp.tile(g_mq, MEM_HEADS).reshape(1, MEM_WIDTH), w_s, bs_full, _group_block())


def kernel(x, mem, norm_ffn1, ffn1_w_in, ffn1_w_out, norm_mix, norm_ffn2, ffn2_w_in, ffn2_w_out,
           w_out, mem_norm, mem_w_kv, mem_q_norm, mem_k_norm, fox_w_in, fox_b_f, fox_q_norm,
           fox_k_norm, gmlp_w_in, gmlp_v_norm, gmlp_w_s, gmlp_b_s):
    b, s, d = x.shape
    depth = norm_mix.shape[0]
    for i in range(depth):
        kind, j = i % 2, i // 2
        x = _ffn(x.reshape(b * s, d), norm_ffn1[i], ffn1_w_in[i], ffn1_w_out[i]).reshape(b, s, d)
        km, vm = _mem_kv(mem, mem_norm, mem_w_kv[i], mem_k_norm[i])
        if kind == 0:
            q, k, v, mq = _fox_in(x, norm_mix[i], fox_w_in[j], fox_b_f[j], fox_q_norm[j],
                                  fox_k_norm[j], mem_q_norm[i])
            tok = _fox_attn(q, k, v)
        else:
            tok, mq = _gmlp_in(x, norm_mix[i], gmlp_w_in[j], gmlp_v_norm[j], gmlp_w_s[j],
                               gmlp_b_s[j], mem_q_norm[i])
        x = _mix_out(x, tok, mq, km, vm, w_out[i])
        x = _ffn(x.reshape(b * s, d), norm_ffn2[i], ffn2_w_in[i], ffn2_w_out[i]).reshape(b, s, d)
    return x
```

```python
import math

import jax
import jax.numpy as jnp
from jax import lax
from jax.experimental import pallas as pl
from jax.experimental.pallas import tpu as pltpu

D_MODEL = 1024
D_FF = 2816
HEAD_DIM = 64
MEM_LEN = 256
MEM_HEADS = 4
MEM_WIDTH = MEM_HEADS * HEAD_DIM
TOK_WIDTH = D_MODEL - MEM_WIDTH
TOK_HEADS = TOK_WIDTH // HEAD_DIM
HEAD_PAIRS = TOK_HEADS // 2
CHUNK = 128
EPS = 1e-6

LANES = 128
MXU_WIDTH = 256
SUBLANES = 8
PAIR_WIDTH = 2 * HEAD_DIM
NEG_BIG = -1e30
QK_SCALE = 1.0 / math.sqrt(HEAD_DIM)
LOG2E = math.log2(math.e)
SPLIT = 3

FFN_TM = 1024
FFN_FC = 256
MIX_TM = 512
ATT_TQ = 512
ATT_TK = ATT_TQ
VMEM_LIMIT = 56 * 1024 * 1024

KEY_BLOCK = 2 * PAIR_WIDTH
ONES_LANE0 = 2 * SPLIT
VT_BLOCK = 128

F32 = jnp.float32
BF16 = jnp.bfloat16


def _dot(a, b):
    return jnp.dot(a, b, preferred_element_type=F32)


def _dot_nt(a, b):
    return lax.dot_general(a, b, (((1,), (1,)), ((), ())), preferred_element_type=F32)


def _split3(y):
    hi = y.astype(BF16)
    r = y - hi.astype(F32)
    mid = r.astype(BF16)
    lo = (r - mid.astype(F32)).astype(BF16)
    return hi, mid, lo


def _rms_rows(x, gain):
    ms = jnp.mean(x * x, axis=-1, keepdims=True)
    return x * lax.rsqrt(ms + EPS) * gain


def _group_rms(y, p_blk, gain):
    sq = (y * y).astype(BF16)
    ss = jnp.concatenate(
        [_dot(sq[:, c:c + MXU_WIDTH], p_blk) for c in range(0, y.shape[1], MXU_WIDTH)], axis=1)
    return y * lax.rsqrt(ss * (1.0 / HEAD_DIM) + EPS) * gain


def _group_block():
    grp = jnp.arange(MXU_WIDTH) // HEAD_DIM
    return (grp[:, None] == grp[None, :]).astype(BF16)


def _const_spec(shape):
    nd = len(shape)
    return pl.BlockSpec(shape, lambda *_: (0,) * nd, pipeline_mode=pl.Buffered(1))


def _params(sem):
    return pltpu.CompilerParams(dimension_semantics=sem, vmem_limit_bytes=VMEM_LIMIT)


def _ffn_kernel(x_ref, g_ref, wa_ref, wb_ref, wo_ref, o_ref, h_scr, g_scr):
    h_scr[...] = _rms_rows(x_ref[...], g_ref[...]).astype(BF16)
    for c in range(D_FF // FFN_FC):
        cols = slice(c * FFN_FC, (c + 1) * FFN_FC)
        h = h_scr[...]
        a = _dot(h, wa_ref[:, cols])
        b = _dot(h, wb_ref[:, cols])
        g_scr[:, cols] = (a * jax.nn.sigmoid(a) * b).astype(BF16)
    o_ref[...] = x_ref[...] + 0.5 * _dot(g_scr[...], wo_ref[...])


def _ffn(x2, gain, w_in, w_out):
    n = x2.shape[0]
    wa = w_in[:, :D_FF].astype(BF16)
    wb = w_in[:, D_FF:].astype(BF16)
    wo = w_out.astype(BF16)
    return pl.pallas_call(
        _ffn_kernel,
        grid=(n // FFN_TM,),
        in_specs=[
            pl.BlockSpec((FFN_TM, D_MODEL), lambda i: (i, 0)),
            _const_spec((1, D_MODEL)),
            _const_spec((D_MODEL, D_FF)),
            _const_spec((D_MODEL, D_FF)),
            _const_spec((D_FF, D_MODEL)),
        ],
        out_specs=pl.BlockSpec((FFN_TM, D_MODEL), lambda i: (i, 0)),
        out_shape=jax.ShapeDtypeStruct((n, D_MODEL), F32),
        scratch_shapes=[pltpu.VMEM((FFN_TM, D_MODEL), BF16),
                        pltpu.VMEM((FFN_TM, D_FF), BF16)],
        compiler_params=_params(("arbitrary",)),
        name="ffn",
    )(x2, gain.reshape(1, D_MODEL), wa, wb, wo)


def _fox_in_kernel(x_ref, g_ref, w_ref, wvt_ref, bf_ref, gq_ref, gk_ref, gmq_ref,
                   gb_ref, tri_ref, route_ref, ones_ref,
                   q_ref, k_ref, vt_ref, mq_ref, carry_scr):
    @pl.when(pl.program_id(1) == 0)
    def _():
        carry_scr[...] = jnp.zeros_like(carry_scr)

    t = TOK_WIDTH
    h = _rms_rows(x_ref[0], g_ref[...]).astype(BF16)
    f = _dot(h, w_ref[:, 2 * t + MEM_WIDTH:]) + bf_ref[...]
    q = _dot(h, w_ref[:, 0:t])
    log_f = (jnp.minimum(f, 0.0) - jnp.log1p(jnp.exp(-jnp.abs(f)))) * LOG2E
    k = _dot(h, w_ref[:, t:2 * t])
    tri = tri_ref[...]
    c3 = _dot(tri, jnp.concatenate(_split3(log_f), axis=1))
    c = sum(c3[:, j * LANES:(j + 1) * LANES] for j in range(SPLIT)) + carry_scr[0:1, :]
    carry_scr[...] = jnp.broadcast_to(c[MIX_TM - 1:MIX_TM, :], carry_scr.shape)
    vt = _dot_nt(wvt_ref[...], h)
    mq = _dot(h, w_ref[:, 2 * t:2 * t + MEM_WIDTH])
    q_ref[0] = (_group_rms(q, gb_ref[...], gq_ref[...])
                * (QK_SCALE * LOG2E)).astype(BF16)
    decay = _dot(jnp.concatenate(_split3(c), axis=1), route_ref[...]) + ones_ref[...]
    kn = _group_rms(k, gb_ref[...], gk_ref[...])
    mq_ref[0] = (_group_rms(mq, gb_ref[...], gmq_ref[...]) * QK_SCALE).astype(BF16)

    for p in range(HEAD_PAIRS):
        pair = slice(p * PAIR_WIDTH, (p + 1) * PAIR_WIDTH)
        k_ref[0, :, p * KEY_BLOCK:p * KEY_BLOCK + PAIR_WIDTH] = kn[:, pair].astype(BF16)
        k_ref[0, :, p * KEY_BLOCK + PAIR_WIDTH:(p + 1) * KEY_BLOCK] = decay[:, pair].astype(BF16)
    ones_rows = jnp.where(
        lax.broadcasted_iota(jnp.int32, (VT_BLOCK - HEAD_DIM, MIX_TM), 0) == 0, 1.0, 0.0)
    for hd in range(TOK_HEADS):
        vt_ref[0, hd * VT_BLOCK:hd * VT_BLOCK + HEAD_DIM, :] = (
            vt[hd * HEAD_DIM:(hd + 1) * HEAD_DIM, :].astype(BF16))
        vt_ref[0, hd * VT_BLOCK + HEAD_DIM:(hd + 1) * VT_BLOCK, :] = ones_rows.astype(BF16)


def _decay_route():
    r = jnp.zeros((SPLIT * LANES, HEAD_PAIRS * PAIR_WIDTH), F32)
    for hd in range(TOK_HEADS):
        for j in range(SPLIT):
            r = r.at[j * LANES + hd, (hd // 2) * PAIR_WIDTH + (hd % 2) * SPLIT + j].set(-1.0)
    ones = jnp.zeros((1, HEAD_PAIRS * PAIR_WIDTH), F32)
    for p in range(HEAD_PAIRS):
        ones = ones.at[0, p * PAIR_WIDTH + ONES_LANE0:p * PAIR_WIDTH + ONES_LANE0 + SPLIT].set(1.0)
    return r.astype(BF16), ones


def _fox_in(x, gain, w_in, b_f, g_q, g_k, g_mq):
    b, s, _ = x.shape
    t = TOK_WIDTH
    n_gate = 3 * t
    w_all = jnp.concatenate(
        [w_in[:, :2 * t], w_in[:, n_gate + TOK_HEADS:], w_in[:, n_gate:n_gate + TOK_HEADS],
         jnp.zeros((D_MODEL, LANES - TOK_HEADS), w_in.dtype)], axis=1).astype(BF16)
    n_all = w_all.shape[1]
    wvt = w_in[:, 2 * t:n_gate].T.astype(BF16)
    bf = jnp.zeros((1, LANES), F32).at[0, :TOK_HEADS].set(b_f)
    idx = jnp.arange(MIX_TM)
    tri = (idx[None, :] <= idx[:, None]).astype(BF16)
    route, ones = _decay_route()
    tile = lambda w: pl.BlockSpec((1, MIX_TM, w), lambda bi, i: (bi, i, 0))
    return pl.pallas_call(
        _fox_in_kernel,
        grid=(b, s // MIX_TM),
        in_specs=[
            tile(D_MODEL),
            _const_spec((1, D_MODEL)),
            _const_spec((D_MODEL, n_all)),
            _const_spec((t, D_MODEL)),
            _const_spec((1, LANES)),
            _const_spec((1, t)),
            _const_spec((1, t)),
            _const_spec((1, MEM_WIDTH)),
            _const_spec((MXU_WIDTH, MXU_WIDTH)),
            _const_spec((MIX_TM, MIX_TM)),
            _const_spec(route.shape),
            _const_spec(ones.shape),
        ],
        out_specs=[tile(t), tile(HEAD_PAIRS * KEY_BLOCK),
                   pl.BlockSpec((1, TOK_HEADS * VT_BLOCK, MIX_TM), lambda bi, i: (bi, 0, i)),
                   tile(MEM_WIDTH)],
        out_shape=[jax.ShapeDtypeStruct((b, s, t), BF16),
                   jax.ShapeDtypeStruct((b, s, HEAD_PAIRS * KEY_BLOCK), BF16),
                   jax.ShapeDtypeStruct((b, TOK_HEADS * VT_BLOCK, s), BF16),
                   jax.ShapeDtypeStruct((b, s, MEM_WIDTH), BF16)],
        scratch_shapes=[pltpu.VMEM((SUBLANES, LANES), F32)],
        compiler_params=_params(("arbitrary", "arbitrary")),
        name="fox_in",
    )(x, gain.reshape(1, D_MODEL), w_all, wvt, bf,
      jnp.tile(g_q, TOK_HEADS).reshape(1, t), jnp.tile(g_k, TOK_HEADS).reshape(1, t),
      jnp.tile(g_mq, MEM_HEADS).reshape(1, MEM_WIDTH), _group_block(), tri, route, ones)


def _fox_attn_kernel(q_ref, k_ref, vt_ref, o_ref, qa_scr, sa_scr, sb_scr, m_scr, acc_scr):
    i = pl.program_id(2)
    q0 = pl.multiple_of(i * ATT_TQ, ATT_TQ)
    lane = lax.broadcasted_iota(jnp.int32, (ATT_TQ, PAIR_WIDTH), 1)
    first = lane < HEAD_DIM
    q = q_ref[0]
    zero = jnp.zeros_like(q)
    base = -k_ref[0, pl.ds(q0, 2 * SUBLANES), PAIR_WIDTH:KEY_BLOCK].astype(F32)
    base = jnp.broadcast_to(base[0:1, :], (ATT_TQ, PAIR_WIDTH))
    in_ones = (lane >= ONES_LANE0) & (lane < ONES_LANE0 + SPLIT)
    for hh in range(2):
        own = (lane >= hh * SPLIT) & (lane < (hh + 1) * SPLIT)
        moved = pltpu.roll(base, ONES_LANE0 - hh * SPLIT, 1)
        qa = jnp.where(own, 1.0, jnp.where(in_ones, moved, 0.0))
        qa_scr[hh, :, 0:PAIR_WIDTH] = jnp.where(first == (hh == 0), q, zero)
        qa_scr[hh, :, PAIR_WIDTH:KEY_BLOCK] = qa.astype(BF16)
    m_scr[...] = jnp.full_like(m_scr, NEG_BIG)
    acc_scr[...] = jnp.zeros_like(acc_scr)

    def scores(blk, s_scr):
        kb = k_ref[0, pl.ds(pl.multiple_of(blk * ATT_TK, ATT_TK), ATT_TK), :]
        for hh in range(2):
            s_scr[hh] = _dot_nt(kb, qa_scr[hh])

    def consume(blk, s_scr, masked):
        k0 = pl.multiple_of(blk * ATT_TK, ATT_TK)
        probs = []
        for hh in range(2):
            s = s_scr[hh]
            if masked:
                key = lax.broadcasted_iota(jnp.int32, (ATT_TK, ATT_TQ), 0)
                qry = lax.broadcasted_iota(jnp.int32, (ATT_TK, ATT_TQ), 1)
                s = jnp.where(key <= qry, s, NEG_BIG)
            m_prev = m_scr[hh, 0:1, :]
            m_new = jnp.maximum(m_prev, jnp.max(s, axis=0, keepdims=True))
            m_scr[hh] = jnp.broadcast_to(m_new, (SUBLANES, ATT_TQ))
            probs.append((jnp.exp2(s - m_new).astype(BF16), jnp.exp2(m_prev - m_new)))
        for hh in range(2):
            p, alpha = probs[hh]
            vt = vt_ref[0, hh * VT_BLOCK:(hh + 1) * VT_BLOCK, pl.ds(k0, ATT_TK)]
            acc_scr[hh] = alpha * acc_scr[hh] + _dot(vt, p)

    scores(0, sa_scr)

    def body(jj, carry):
        scores(2 * jj + 1, sb_scr)
        consume(2 * jj, sa_scr, False)
        scores(2 * jj + 2, sa_scr)
        consume(2 * jj + 1, sb_scr, False)
        return carry

    lax.fori_loop(0, i // 2, body, 0)

    @pl.when(i % 2 == 0)
    def _():
        consume(i, sa_scr, True)

    @pl.when(i % 2 == 1)
    def _():
        scores(i, sb_scr)
        consume(i - 1, sa_scr, False)
        consume(i, sb_scr, True)

    halves = []
    for hh in range(2):
        acc = acc_scr[hh]
        halves.append(acc[0:HEAD_DIM, :] / acc[HEAD_DIM:HEAD_DIM + 1, :])
    o_ref[0] = jnp.concatenate(halves, axis=0).T.astype(o_ref.dtype)


def _fox_attn(q, k, v):
    b, s, t = q.shape
    return pl.pallas_call(
        _fox_attn_kernel,
        grid=(b, HEAD_PAIRS, s // ATT_TQ),
        in_specs=[
            pl.BlockSpec((1, ATT_TQ, PAIR_WIDTH), lambda bi, p, i: (bi, i, p)),
            pl.BlockSpec((1, s, KEY_BLOCK), lambda bi, p, i: (bi, 0, p)),
            pl.BlockSpec((1, 2 * VT_BLOCK, s), lambda bi, p, i: (bi, p, 0)),
        ],
        out_specs=pl.BlockSpec((1, ATT_TQ, PAIR_WIDTH), lambda bi, p, i: (bi, i, p)),
        out_shape=jax.ShapeDtypeStruct((b, s, t), BF16),
        scratch_shapes=[pltpu.VMEM((2, ATT_TQ, KEY_BLOCK), BF16),
                        pltpu.VMEM((2, ATT_TK, ATT_TQ), F32),
                        pltpu.VMEM((2, ATT_TK, ATT_TQ), F32),
                        pltpu.VMEM((2, SUBLANES, ATT_TQ), F32),
                        pltpu.VMEM((2, VT_BLOCK, ATT_TQ), F32)],
        compiler_params=_params(("arbitrary", "arbitrary", "arbitrary")),
        name="fox_attn",
    )(q, k, v)


def _mem_kv_kernel(mem_ref, g_ref, w_ref, gk_ref, gb_ref, k_ref, v_ref):
    mem_n = _rms_rows(mem_ref[0], g_ref[...]).astype(BF16)
    kv = _dot(mem_n, w_ref[...])
    k_ref[0] = _group_rms(kv[:, :MEM_WIDTH], gb_ref[...], gk_ref[...]).astype(BF16)
    v_ref[0] = kv[:, MEM_WIDTH:].astype(BF16)


def _mem_kv(mem, g_mem, w_kv, g_k):
    b = mem.shape[0]
    blk = pl.BlockSpec((1, MEM_LEN, MEM_WIDTH), lambda bi: (bi, 0, 0))
    return pl.pallas_call(
        _mem_kv_kernel,
        grid=(b,),
        in_specs=[
            pl.BlockSpec((1, MEM_LEN, D_MODEL), lambda bi: (bi, 0, 0)),
            _const_spec((1, D_MODEL)),
            _const_spec((D_MODEL, 2 * MEM_WIDTH)),
            _const_spec((1, MEM_WIDTH)),
            _const_spec((MXU_WIDTH, MXU_WIDTH)),
        ],
        out_specs=[blk, blk],
        out_shape=[jax.ShapeDtypeStruct((b, MEM_LEN, MEM_WIDTH), BF16)] * 2,
        compiler_params=_params(("arbitrary",)),
        name="mem_kv",
    )(mem, g_mem.reshape(1, D_MODEL), w_kv.astype(BF16),
      jnp.tile(g_k, MEM_HEADS).reshape(1, MEM_WIDTH), _group_block())


def _mix_out_kernel(x_ref, tok_ref, mq_ref, km_ref, vm_ref, wt_ref, wm_ref, o_ref):
    mq = mq_ref[0]
    km = km_ref[0]
    vm = vm_ref[0]
    lane = lax.broadcasted_iota(jnp.int32, (MIX_TM, MEM_WIDTH), 1) // HEAD_DIM
    zero = jnp.zeros_like(mq)
    scores = [_dot_nt(jnp.where(lane == hh, mq, zero), km) for hh in range(MEM_HEADS)]
    y = _dot(tok_ref[0], wt_ref[...])
    probs = []
    for s in scores:
        p = jnp.exp(s - jnp.max(s, axis=-1, keepdims=True))
        probs.append((p / jnp.sum(p, axis=-1, keepdims=True)).astype(BF16))
    mo = jnp.zeros((MIX_TM, MEM_WIDTH), F32)
    for hh in range(MEM_HEADS):
        mo = jnp.where(lane == hh, _dot(probs[hh], vm), mo)
    o_ref[0] = x_ref[0] + y + _dot(mo.astype(BF16), wm_ref[...])


def _mix_out(x, tok, mq, km, vm, w_out):
    b, s, _ = x.shape
    wt = w_out[:TOK_WIDTH].astype(BF16)
    wm = w_out[TOK_WIDTH:].astype(BF16)
    tile = lambda w: pl.BlockSpec((1, MIX_TM, w), lambda bi, i: (bi, i, 0))
    memblk = pl.BlockSpec((1, MEM_LEN, MEM_WIDTH), lambda bi, i: (bi, 0, 0))
    return pl.pallas_call(
        _mix_out_kernel,
        grid=(b, s // MIX_TM),
        in_specs=[tile(D_MODEL), tile(TOK_WIDTH), tile(MEM_WIDTH), memblk, memblk,
                  _const_spec((TOK_WIDTH, D_MODEL)), _const_spec((MEM_WIDTH, D_MODEL))],
        out_specs=tile(D_MODEL),
        out_shape=jax.ShapeDtypeStruct((b, s, D_MODEL), F32),
        compiler_params=_params(("arbitrary", "arbitrary")),
        name="mix_out",
    )(x, tok, mq, km, vm, wt, wm)


def _gelu_tanh(x):
    return 0.5 * x * (1.0 + jnp.tanh(math.sqrt(2.0 / math.pi) * (x + 0.044715 * (x * x * x))))


def _gmlp_in_kernel(x_ref, g_ref, wuv_ref, wmq_ref, gv_ref, gmq_ref, ws_ref, bs_ref,
                    gb_ref, tok_ref, mq_ref, vn_scr, u_scr):
    h = _rms_rows(x_ref[0], g_ref[...]).astype(BF16)
    t = TOK_WIDTH
    n_chunk = MIX_TM // CHUNK
    v = _dot(h, wuv_ref[:, t:2 * t])
    mq = _dot(h, wmq_ref[...])
    u_scr[...] = _gelu_tanh(_dot(h, wuv_ref[:, 0:t]))
    vn_scr[...] = _group_rms(_gelu_tanh(v), gb_ref[...], gv_ref[...]).astype(BF16)
    mq_ref[0] = (_group_rms(mq, gb_ref[...], gmq_ref[...]) * QK_SCALE).astype(BF16)

    row = lax.broadcasted_iota(jnp.int32, (CHUNK, CHUNK), 0)
    col = lax.broadcasted_iota(jnp.int32, (CHUNK, CHUNK), 1)
    causal = col <= row
    first = lax.broadcasted_iota(jnp.int32, (CHUNK, PAIR_WIDTH), 1) < HEAD_DIM
    for p in range(HEAD_PAIRS):
        lanes = slice(p * PAIR_WIDTH, (p + 1) * PAIR_WIDTH)
        v_cat = jnp.concatenate(
            [vn_scr[c * CHUNK:(c + 1) * CHUNK, lanes] for c in range(n_chunk)], axis=1)
        wa = jnp.where(causal, ws_ref[2 * p], 0.0).astype(BF16)
        wb = jnp.where(causal, ws_ref[2 * p + 1], 0.0).astype(BF16)
        ga = _dot(wa, v_cat)
        gb = _dot(wb, v_cat)
        bias = bs_ref[:, lanes]
        for c in range(n_chunk):
            cols = slice(c * PAIR_WIDTH, (c + 1) * PAIR_WIDTH)
            gate = jnp.where(first, ga[:, cols], gb[:, cols]) + bias
            rows = slice(c * CHUNK, (c + 1) * CHUNK)
            tok_ref[0, rows, lanes] = (u_scr[rows, lanes] * gate).astype(BF16)


def _gmlp_in(x, gain, w_in, v_gain, w_s, b_s, g_mq):
    b, s, _ = x.shape
    t = TOK_WIDTH
    wuv = w_in[:, :2 * t].astype(BF16)
    wmq = w_in[:, 2 * t:].astype(BF16)
    bs_full = jnp.repeat(b_s.T, HEAD_DIM, axis=1)
    tile = lambda w: pl.BlockSpec((1, MIX_TM, w), lambda bi, i: (bi, i, 0))
    return pl.pallas_call(
        _gmlp_in_kernel,
        grid=(b, s // MIX_TM),
        in_specs=[
            tile(D_MODEL),
            _const_spec((1, D_MODEL)),
            _const_spec((D_MODEL, 2 * t)),
            _const_spec((D_MODEL, MEM_WIDTH)),
            _const_spec((1, t)),
            _const_spec((1, MEM_WIDTH)),
            _const_spec((TOK_HEADS, CHUNK, CHUNK)),
            _const_spec((CHUNK, t)),
            _const_spec((MXU_WIDTH, MXU_WIDTH)),
        ],
        out_specs=[tile(t), tile(MEM_WIDTH)],
        out_shape=[jax.ShapeDtypeStruct((b, s, t), BF16),
                   jax.ShapeDtypeStruct((b, s, MEM_WIDTH), BF16)],
        scratch_shapes=[pltpu.VMEM((MIX_TM, t), BF16), pltpu.VMEM((MIX_TM, t), F32)],
        compiler_params=_params(("arbitrary", "arbitrary")),
        name="gmlp_in",
    )(x, gain.reshape(1, D_MODEL), wuv, wmq, v_gain.reshape(1, t),
      jnp.tile(g_mq, MEM_HEADS).reshape(1, MEM_WIDTH), w_s, bs_full, _group_block())


def kernel(x, mem, norm_ffn1, ffn1_w_in, ffn1_w_out, norm_mix, norm_ffn2, ffn2_w_in, ffn2_w_out,
           w_out, mem_norm, mem_w_kv, mem_q_norm, mem_k_norm, fox_w_in, fox_b_f, fox_q_norm,
           fox_k_norm, gmlp_w_in, gmlp_v_norm, gmlp_w_s, gmlp_b_s):
    b, s, d = x.shape
    depth = norm_mix.shape[0]
    for i in range(depth):
        kind, j = i % 2, i // 2
        x = _ffn(x.reshape(b * s, d), norm_ffn1[i], ffn1_w_in[i], ffn1_w_out[i]).reshape(b, s, d)
        km, vm = _mem_kv(mem, mem_norm, mem_w_kv[i], mem_k_norm[i])
        if kind == 0:
            q, k, v, mq = _fox_in(x, norm_mix[i], fox_w_in[j], fox_b_f[j], fox_q_norm[j],
                                  fox_k_norm[j], mem_q_norm[i])
            tok = _fox_attn(q, k, v)
        else:
            tok, mq = _gmlp_in(x, norm_mix[i], gmlp_w_in[j], gmlp_v_norm[j], gmlp_w_s[j],
                               gmlp_b_s[j], mem_q_norm[i])
        x = _mix_out(x, tok, mq, km, vm, w_out[i])
        x = _ffn(x.reshape(b * s, d), norm_ffn2[i], ffn2_w_in[i], ffn2_w_out[i]).reshape(b, s, d)
    return x
```

```python
import math

import jax
import jax.numpy as jnp
from jax import lax
from jax.experimental import pallas as pl
from jax.experimental.pallas import tpu as pltpu

D_MODEL = 1024
D_FF = 2816
HEAD_DIM = 64
MEM_LEN = 256
MEM_HEADS = 4
MEM_WIDTH = MEM_HEADS * HEAD_DIM
TOK_WIDTH = D_MODEL - MEM_WIDTH
TOK_HEADS = TOK_WIDTH // HEAD_DIM
HEAD_PAIRS = TOK_HEADS // 2
CHUNK = 128
EPS = 1e-6

LANES = 128
MXU_WIDTH = 256
SUBLANES = 8
PAIR_WIDTH = 2 * HEAD_DIM
NEG_BIG = -1e30
QK_SCALE = 1.0 / math.sqrt(HEAD_DIM)
LOG2E = math.log2(math.e)
SPLIT = 3

FFN_TM = 1024
FFN_FC = 256
MIX_TM = 512
ATT_TQ = 512
ATT_TK = ATT_TQ
VMEM_LIMIT = 56 * 1024 * 1024

KEY_BLOCK = 2 * PAIR_WIDTH
ONES_LANE0 = 2 * SPLIT
VT_BLOCK = 128

F32 = jnp.float32
BF16 = jnp.bfloat16


def _dot(a, b):
    return jnp.dot(a, b, preferred_element_type=F32)


def _dot_nt(a, b):
    return lax.dot_general(a, b, (((1,), (1,)), ((), ())), preferred_element_type=F32)


def _split3(y):
    hi = y.astype(BF16)
    r = y - hi.astype(F32)
    mid = r.astype(BF16)
    lo = (r - mid.astype(F32)).astype(BF16)
    return hi, mid, lo


def _rms_rows(x, gain):
    ms = jnp.mean(x * x, axis=-1, keepdims=True)
    return x * lax.rsqrt(ms + EPS) * gain


def _group_rms(y, p_blk, gain):
    sq = (y * y).astype(BF16)
    ss = jnp.concatenate(
        [_dot(sq[:, c:c + MXU_WIDTH], p_blk) for c in range(0, y.shape[1], MXU_WIDTH)], axis=1)
    return y * lax.rsqrt(ss * (1.0 / HEAD_DIM) + EPS) * gain


def _group_block():
    grp = jnp.arange(MXU_WIDTH) // HEAD_DIM
    return (grp[:, None] == grp[None, :]).astype(BF16)


def _const_spec(shape):
    nd = len(shape)
    return pl.BlockSpec(shape, lambda *_: (0,) * nd, pipeline_mode=pl.Buffered(1))


def _params(sem):
    return pltpu.CompilerParams(dimension_semantics=sem, vmem_limit_bytes=VMEM_LIMIT)


def _ffn_kernel(x_ref, g_ref, wa_ref, wb_ref, wo_ref, o_ref, h_scr, g_scr):
    h_scr[...] = _rms_rows(x_ref[...], g_ref[...]).astype(BF16)
    for c in range(D_FF // FFN_FC):
        cols = slice(c * FFN_FC, (c + 1) * FFN_FC)
        h = h_scr[...]
        a = _dot(h, wa_ref[:, cols])
        b = _dot(h, wb_ref[:, cols])
        g_scr[:, cols] = (a * jax.nn.sigmoid(a) * b).astype(BF16)
    o_ref[...] = x_ref[...] + 0.5 * _dot(g_scr[...], wo_ref[...])


def _ffn(x2, gain, w_in, w_out):
    n = x2.shape[0]
    wa = w_in[:, :D_FF].astype(BF16)
    wb = w_in[:, D_FF:].astype(BF16)
    wo = w_out.astype(BF16)
    return pl.pallas_call(
        _ffn_kernel,
        grid=(n // FFN_TM,),
        in_specs=[
            pl.BlockSpec((FFN_TM, D_MODEL), lambda i: (i, 0)),
            _const_spec((1, D_MODEL)),
            _const_spec((D_MODEL, D_FF)),
            _const_spec((D_MODEL, D_FF)),
            _const_spec((D_FF, D_MODEL)),
        ],
        out_specs=pl.BlockSpec((FFN_TM, D_MODEL), lambda i: (i, 0)),
        out_shape=jax.ShapeDtypeStruct((n, D_MODEL), F32),
        scratch_shapes=[pltpu.VMEM((FFN_TM, D_MODEL), BF16),
                        pltpu.VMEM((FFN_TM, D_FF), BF16)],
        compiler_params=_params(("arbitrary",)),
        name="ffn",
    )(x2, gain.reshape(1, D_MODEL), wa, wb, wo)


def _fox_in_kernel(x_ref, g_ref, w_ref, wvt_ref, bf_ref, gq_ref, gk_ref, gmq_ref,
                   gb_ref, tri_ref, route_ref, ones_ref,
                   q_ref, k_ref, vt_ref, mq_ref, carry_scr):
    @pl.when(pl.program_id(1) == 0)
    def _():
        carry_scr[...] = jnp.zeros_like(carry_scr)

    t = TOK_WIDTH
    h = _rms_rows(x_ref[0], g_ref[...]).astype(BF16)
    f = _dot(h, w_ref[:, 2 * t + MEM_WIDTH:]) + bf_ref[...]
    q = _dot(h, w_ref[:, 0:t])
    log_f = (jnp.minimum(f, 0.0) - jnp.log1p(jnp.exp(-jnp.abs(f)))) * LOG2E
    k = _dot(h, w_ref[:, t:2 * t])
    tri = tri_ref[...]
    c3 = _dot(tri, jnp.concatenate(_split3(log_f), axis=1))
    c = sum(c3[:, j * LANES:(j + 1) * LANES] for j in range(SPLIT)) + carry_scr[0:1, :]
    carry_scr[...] = jnp.broadcast_to(c[MIX_TM - 1:MIX_TM, :], carry_scr.shape)
    vt = _dot_nt(wvt_ref[...], h)
    mq = _dot(h, w_ref[:, 2 * t:2 * t + MEM_WIDTH])
    q_ref[0] = (_group_rms(q, gb_ref[...], gq_ref[...])
                * (QK_SCALE * LOG2E)).astype(BF16)
    decay = _dot(jnp.concatenate(_split3(c), axis=1), route_ref[...]) + ones_ref[...]
    kn = _group_rms(k, gb_ref[...], gk_ref[...])
    mq_ref[0] = (_group_rms(mq, gb_ref[...], gmq_ref[...]) * QK_SCALE).astype(BF16)

    for p in range(HEAD_PAIRS):
        pair = slice(p * PAIR_WIDTH, (p + 1) * PAIR_WIDTH)
        k_ref[0, :, p * KEY_BLOCK:p * KEY_BLOCK + PAIR_WIDTH] = kn[:, pair].astype(BF16)
        k_ref[0, :, p * KEY_BLOCK + PAIR_WIDTH:(p + 1) * KEY_BLOCK] = decay[:, pair].astype(BF16)
    ones_rows = jnp.where(
        lax.broadcasted_iota(jnp.int32, (VT_BLOCK - HEAD_DIM, MIX_TM), 0) == 0, 1.0, 0.0)
    for hd in range(TOK_HEADS):
        vt_ref[0, hd * VT_BLOCK:hd * VT_BLOCK + HEAD_DIM, :] = (
            vt[hd * HEAD_DIM:(hd + 1) * HEAD_DIM, :].astype(BF16))
        vt_ref[0, hd * VT_BLOCK + HEAD_DIM:(hd + 1) * VT_BLOCK, :] = ones_rows.astype(BF16)


def _decay_route():
    r = jnp.zeros((SPLIT * LANES, HEAD_PAIRS * PAIR_WIDTH), F32)
    for hd in range(TOK_HEADS):
        for j in range(SPLIT):
            r = r.at[j * LANES + hd, (hd // 2) * PAIR_WIDTH + (hd % 2) * SPLIT + j].set(-1.0)
    ones = jnp.zeros((1, HEAD_PAIRS * PAIR_WIDTH), F32)
    for p in range(HEAD_PAIRS):
        ones = ones.at[0, p * PAIR_WIDTH + ONES_LANE0:p * PAIR_WIDTH + ONES_LANE0 + SPLIT].set(1.0)
    return r.astype(BF16), ones


def _fox_in(x, gain, w_in, b_f, g_q, g_k, g_mq):
    b, s, _ = x.shape
    t = TOK_WIDTH
    n_gate = 3 * t
    w_all = jnp.concatenate(
        [w_in[:, :2 * t], w_in[:, n_gate + TOK_HEADS:], w_in[:, n_gate:n_gate + TOK_HEADS],
         jnp.zeros((D_MODEL, LANES - TOK_HEADS), w_in.dtype)], axis=1).astype(BF16)
    n_all = w_all.shape[1]
    wvt = w_in[:, 2 * t:n_gate].T.astype(BF16)
    bf = jnp.zeros((1, LANES), F32).at[0, :TOK_HEADS].set(b_f)
    idx = jnp.arange(MIX_TM)
    tri = (idx[None, :] <= idx[:, None]).astype(BF16)
    route, ones = _decay_route()
    tile = lambda w: pl.BlockSpec((1, MIX_TM, w), lambda bi, i: (bi, i, 0))
    return pl.pallas_call(
        _fox_in_kernel,
        grid=(b, s // MIX_TM),
        in_specs=[
            tile(D_MODEL),
            _const_spec((1, D_MODEL)),
            _const_spec((D_MODEL, n_all)),
            _const_spec((t, D_MODEL)),
            _const_spec((1, LANES)),
            _const_spec((1, t)),
            _const_spec((1, t)),
            _const_spec((1, MEM_WIDTH)),
            _const_spec((MXU_WIDTH, MXU_WIDTH)),
            _const_spec((MIX_TM, MIX_TM)),
            _const_spec(route.shape),
            _const_spec(ones.shape),
        ],
        out_specs=[tile(t), tile(HEAD_PAIRS * KEY_BLOCK),
                   pl.BlockSpec((1, TOK_HEADS * VT_BLOCK, MIX_TM), lambda bi, i: (bi, 0, i)),
                   tile(MEM_WIDTH)],
        out_shape=[jax.ShapeDtypeStruct((b, s, t), BF16),
                   jax.ShapeDtypeStruct((b, s, HEAD_PAIRS * KEY_BLOCK), BF16),
                   jax.ShapeDtypeStruct((b, TOK_HEADS * VT_BLOCK, s), BF16),
                   jax.ShapeDtypeStruct((b, s, MEM_WIDTH), BF16)],
        scratch_shapes=[pltpu.VMEM((SUBLANES, LANES), F32)],
        compiler_params=_params(("arbitrary", "arbitrary")),
        name="fox_in",
    )(x, gain.reshape(1, D_MODEL), w_all, wvt, bf,
      jnp.tile(g_q, TOK_HEADS).reshape(1, t), jnp.tile(g_k, TOK_HEADS).reshape(1, t),
      jnp.tile(g_mq, MEM_HEADS).reshape(1, MEM_WIDTH), _group_block(), tri, route, ones)


def _fox_attn_kernel(q_ref, k_ref, vt_ref, o_ref, qa_scr, sa_scr, sb_scr, m_scr, acc_scr):
    def tile(i, carry):
        q0 = pl.multiple_of(i * ATT_TQ, ATT_TQ)
        lane = lax.broadcasted_iota(jnp.int32, (ATT_TQ, PAIR_WIDTH), 1)
        first = lane < HEAD_DIM
        q = q_ref[0, pl.ds(q0, ATT_TQ), :].astype(F32)
        zero = jnp.zeros_like(q)
        base = -k_ref[0, pl.ds(q0, 2 * SUBLANES), PAIR_WIDTH:KEY_BLOCK].astype(F32)
        base = jnp.broadcast_to(base[0:1, :], (ATT_TQ, PAIR_WIDTH))
        in_ones = (lane >= ONES_LANE0) & (lane < ONES_LANE0 + SPLIT)
        for hh in range(2):
            own = (lane >= hh * SPLIT) & (lane < (hh + 1) * SPLIT)
            moved = pltpu.roll(base, ONES_LANE0 - hh * SPLIT, 1)
            qa = jnp.where(own, 1.0, jnp.where(in_ones, moved, 0.0))
            qa_scr[hh, 0:PAIR_WIDTH, :] = jnp.where(first == (hh == 0), q, zero).T.astype(BF16)
            qa_scr[hh, PAIR_WIDTH:KEY_BLOCK, :] = qa.T.astype(BF16)
        m_scr[...] = jnp.full_like(m_scr, NEG_BIG)
        acc_scr[...] = jnp.zeros_like(acc_scr)

        def scores(blk, s_scr):
            kb = k_ref[0, pl.ds(pl.multiple_of(blk * ATT_TK, ATT_TK), ATT_TK), :]
            for hh in range(2):
                s_scr[hh] = _dot(kb, qa_scr[hh])

        def consume(blk, s_scr, masked):
            k0 = pl.multiple_of(blk * ATT_TK, ATT_TK)
            probs = []
            for hh in range(2):
                s = s_scr[hh]
                if masked:
                    key = lax.broadcasted_iota(jnp.int32, (ATT_TK, ATT_TQ), 0)
                    qry = lax.broadcasted_iota(jnp.int32, (ATT_TK, ATT_TQ), 1)
                    s = jnp.where(key <= qry, s, NEG_BIG)
                m_prev = m_scr[hh, 0:1, :]
                m_new = jnp.maximum(m_prev, jnp.max(s, axis=0, keepdims=True))
                m_scr[hh] = jnp.broadcast_to(m_new, (SUBLANES, ATT_TQ))
                probs.append((jnp.exp2(s - m_new).astype(BF16), jnp.exp2(m_prev - m_new)))
            for hh in range(2):
                p, alpha = probs[hh]
                vt = vt_ref[0, hh * VT_BLOCK:(hh + 1) * VT_BLOCK, pl.ds(k0, ATT_TK)]
                acc_scr[hh] = alpha * acc_scr[hh] + _dot(vt, p)

        scores(0, sa_scr)

        def body(jj, inner):
            scores(2 * jj + 1, sb_scr)
            consume(2 * jj, sa_scr, False)
            scores(2 * jj + 2, sa_scr)
            consume(2 * jj + 1, sb_scr, False)
            return inner

        lax.fori_loop(0, i // 2, body, 0)

        @pl.when(i % 2 == 0)
        def _():
            consume(i, sa_scr, True)

        @pl.when(i % 2 == 1)
        def _():
            scores(i, sb_scr)
            consume(i - 1, sa_scr, False)
            consume(i, sb_scr, True)

        halves = []
        for hh in range(2):
            acc = acc_scr[hh]
            halves.append(acc[0:HEAD_DIM, :] / acc[HEAD_DIM:HEAD_DIM + 1, :])
        o_ref[0, pl.ds(q0, ATT_TQ), :] = jnp.concatenate(halves, axis=0).T.astype(o_ref.dtype)
        return carry

    lax.fori_loop(0, q_ref.shape[1] // ATT_TQ, tile, 0)


def _fox_attn(q, k, v):
    b, s, t = q.shape
    return pl.pallas_call(
        _fox_attn_kernel,
        grid=(b, HEAD_PAIRS),
        in_specs=[
            pl.BlockSpec((1, s, PAIR_WIDTH), lambda bi, p: (bi, 0, p)),
            pl.BlockSpec((1, s, KEY_BLOCK), lambda bi, p: (bi, 0, p)),
            pl.BlockSpec((1, 2 * VT_BLOCK, s), lambda bi, p: (bi, p, 0)),
        ],
        out_specs=pl.BlockSpec((1, s, PAIR_WIDTH), lambda bi, p: (bi, 0, p)),
        out_shape=jax.ShapeDtypeStruct((b, s, t), BF16),
        scratch_shapes=[pltpu.VMEM((2, KEY_BLOCK, ATT_TQ), BF16),
                        pltpu.VMEM((2, ATT_TK, ATT_TQ), F32),
                        pltpu.VMEM((2, ATT_TK, ATT_TQ), F32),
                        pltpu.VMEM((2, SUBLANES, ATT_TQ), F32),
                        pltpu.VMEM((2, VT_BLOCK, ATT_TQ), F32)],
        compiler_params=_params(("arbitrary", "arbitrary")),
        name="fox_attn",
    )(q, k, v)


def _mem_kv_kernel(mem_ref, g_ref, w_ref, gk_ref, gb_ref, k_ref, v_ref):
    mem_n = _rms_rows(mem_ref[0], g_ref[...]).astype(BF16)
    kv = _dot(mem_n, w_ref[...])
    k_ref[0] = _group_rms(kv[:, :MEM_WIDTH], gb_ref[...], gk_ref[...]).T.astype(BF16)
    v_ref[0] = kv[:, MEM_WIDTH:].astype(BF16)


def _mem_kv(mem, g_mem, w_kv, g_k):
    b = mem.shape[0]
    blk = pl.BlockSpec((1, MEM_LEN, MEM_WIDTH), lambda bi: (bi, 0, 0))
    return pl.pallas_call(
        _mem_kv_kernel,
        grid=(b,),
        in_specs=[
            pl.BlockSpec((1, MEM_LEN, D_MODEL), lambda bi: (bi, 0, 0)),
            _const_spec((1, D_MODEL)),
            _const_spec((D_MODEL, 2 * MEM_WIDTH)),
            _const_spec((1, MEM_WIDTH)),
            _const_spec((MXU_WIDTH, MXU_WIDTH)),
        ],
        out_specs=[blk, blk],
        out_shape=[jax.ShapeDtypeStruct((b, MEM_LEN, MEM_WIDTH), BF16)] * 2,
        compiler_params=_params(("arbitrary",)),
        name="mem_kv",
    )(mem, g_mem.reshape(1, D_MODEL), w_kv.astype(BF16),
      jnp.tile(g_k, MEM_HEADS).reshape(1, MEM_WIDTH), _group_block())


def _mix_out_kernel(x_ref, tok_ref, mq_ref, km_ref, vm_ref, wt_ref, wm_ref, o_ref):
    mq = mq_ref[0]
    km = km_ref[0]
    vm = vm_ref[0]
    lane = lax.broadcasted_iota(jnp.int32, (MIX_TM, MEM_WIDTH), 1) // HEAD_DIM
    zero = jnp.zeros_like(mq)
    scores = [_dot(jnp.where(lane == hh, mq, zero), km) for hh in range(MEM_HEADS)]
    y = _dot(tok_ref[0], wt_ref[...])
    probs = []
    for s in scores:
        p = jnp.exp(s - jnp.max(s, axis=-1, keepdims=True))
        probs.append((p / jnp.sum(p, axis=-1, keepdims=True)).astype(BF16))
    mo = jnp.zeros((MIX_TM, MEM_WIDTH), F32)
    for hh in range(MEM_HEADS):
        mo = jnp.where(lane == hh, _dot(probs[hh], vm), mo)
    o_ref[0] = x_ref[0] + y + _dot(mo.astype(BF16), wm_ref[...])


def _mix_out(x, tok, mq, km, vm, w_out):
    b, s, _ = x.shape
    wt = w_out[:TOK_WIDTH].astype(BF16)
    wm = w_out[TOK_WIDTH:].astype(BF16)
    tile = lambda w: pl.BlockSpec((1, MIX_TM, w), lambda bi, i: (bi, i, 0))
    memblk = pl.BlockSpec((1, MEM_LEN, MEM_WIDTH), lambda bi, i: (bi, 0, 0))
    return pl.pallas_call(
        _mix_out_kernel,
        grid=(b, s // MIX_TM),
        in_specs=[tile(D_MODEL), tile(TOK_WIDTH), tile(MEM_WIDTH), memblk, memblk,
                  _const_spec((TOK_WIDTH, D_MODEL)), _const_spec((MEM_WIDTH, D_MODEL))],
        out_specs=tile(D_MODEL),
        out_shape=jax.ShapeDtypeStruct((b, s, D_MODEL), F32),
        compiler_params=_params(("arbitrary", "arbitrary")),
        name="mix_out",
    )(x, tok, mq, km, vm, wt, wm)


def _gelu_tanh(x):
    return 0.5 * x * (1.0 + jnp.tanh(math.sqrt(2.0 / math.pi) * (x + 0.044715 * (x * x * x))))


def _gmlp_in_kernel(x_ref, g_ref, wuv_ref, wmq_ref, gv_ref, gmq_ref, ws_ref, bs_ref,
                    gb_ref, tok_ref, mq_ref, vn_scr, u_scr):
    h = _rms_rows(x_ref[0], g_ref[...]).astype(BF16)
    t = TOK_WIDTH
    n_chunk = MIX_TM // CHUNK
    v = _dot(h, wuv_ref[:, t:2 * t])
    mq = _dot(h, wmq_ref[...])
    u_scr[...] = _gelu_tanh(_dot(h, wuv_ref[:, 0:t]))
    vn_scr[...] = _group_rms(_gelu_tanh(v), gb_ref[...], gv_ref[...]).astype(BF16)
    mq_ref[0] = (_group_rms(mq, gb_ref[...], gmq_ref[...]) * QK_SCALE).astype(BF16)

    row = lax.broadcasted_iota(jnp.int32, (CHUNK, CHUNK), 0)
    col = lax.broadcasted_iota(jnp.int32, (CHUNK, CHUNK), 1)
    causal = col <= row
    first = lax.broadcasted_iota(jnp.int32, (CHUNK, PAIR_WIDTH), 1) < HEAD_DIM
    for p in range(HEAD_PAIRS):
        lanes = slice(p * PAIR_WIDTH, (p + 1) * PAIR_WIDTH)
        v_cat = jnp.concatenate(
            [vn_scr[c * CHUNK:(c + 1) * CHUNK, lanes] for c in range(n_chunk)], axis=1)
        wa = jnp.where(causal, ws_ref[2 * p], 0.0).astype(BF16)
        wb = jnp.where(causal, ws_ref[2 * p + 1], 0.0).astype(BF16)
        ga = _dot(wa, v_cat)
        gb = _dot(wb, v_cat)
        bias = bs_ref[:, lanes]
        for c in range(n_chunk):
            cols = slice(c * PAIR_WIDTH, (c + 1) * PAIR_WIDTH)
            gate = jnp.where(first, ga[:, cols], gb[:, cols]) + bias
            rows = slice(c * CHUNK, (c + 1) * CHUNK)
            tok_ref[0, rows, lanes] = (u_scr[rows, lanes] * gate).astype(BF16)


def _gmlp_in(x, gain, w_in, v_gain, w_s, b_s, g_mq):
    b, s, _ = x.shape
    t = TOK_WIDTH
    wuv = w_in[:, :2 * t].astype(BF16)
    wmq = w_in[:, 2 * t:].astype(BF16)
    bs_full = jnp.repeat(b_s.T, HEAD_DIM, axis=1)
    tile = lambda w: pl.BlockSpec((1, MIX_TM, w), lambda bi, i: (bi, i, 0))
    return pl.pallas_call(
        _gmlp_in_kernel,
        grid=(b, s // MIX_TM),
        in_specs=[
            tile(D_MODEL),
            _const_spec((1, D_MODEL)),
            _const_spec((D_MODEL, 2 * t)),
            _const_spec((D_MODEL, MEM_WIDTH)),
            _const_spec((1, t)),
            _const_spec((1, MEM_WIDTH)),
            _const_spec((TOK_HEADS, CHUNK, CHUNK)),
            _const_spec((CHUNK, t)),
            _const_spec((MXU_WIDTH, MXU_WIDTH)),
        ],
        out_specs=[tile(t), tile(MEM_WIDTH)],
        out_shape=[jax.ShapeDtypeStruct((b, s, t), BF16),
                   jax.ShapeDtypeStruct((b, s, MEM_WIDTH), BF16)],
        scratch_shapes=[pltpu.VMEM((MIX_TM, t), BF16), pltpu.VMEM((MIX_TM, t), F32)],
        compiler_params=_params(("arbitrary", "arbitrary")),
        name="gmlp_in",
    )(x, gain.reshape(1, D_MODEL), wuv, wmq, v_gain.reshape(1, t),
      jnp.tile(g_mq, MEM_HEADS).reshape(1, MEM_WIDTH), w_s, bs_full, _group_block())


def kernel(x, mem, norm_ffn1, ffn1_w_in, ffn1_w_out, norm_mix, norm_ffn2, ffn2_w_in, ffn2_w_out,
           w_out, mem_norm, mem_w_kv, mem_q_norm, mem_k_norm, fox_w_in, fox_b_f, fox_q_norm,
           fox_k_norm, gmlp_w_in, gmlp_v_norm, gmlp_w_s, gmlp_b_s):
    b, s, d = x.shape
    depth = norm_mix.shape[0]
    for i in range(depth):
        kind, j = i % 2, i // 2
        x = _ffn(x.reshape(b * s, d), norm_ffn1[i], ffn1_w_in[i], ffn1_w_out[i]).reshape(b, s, d)
        km, vm = _mem_kv(mem, mem_norm, mem_w_kv[i], mem_k_norm[i])
        if kind == 0:
            q, k, v, mq = _fox_in(x, norm_mix[i], fox_w_in[j], fox_b_f[j], fox_q_norm[j],
                                  fox_k_norm[j], mem_q_norm[i])
            tok = _fox_attn(q, k, v)
        else:
            tok, mq = _gmlp_in(x, norm_mix[i], gmlp_w_in[j], gmlp_v_norm[j], gmlp_w_s[j],
                               gmlp_b_s[j], mem_q_norm[i])
        x = _mix_out(x, tok, mq, km, vm, w_out[i])
        x = _ffn(x.reshape(b * s, d), norm_ffn2[i], ffn2_w_in[i], ffn2_w_out[i]).reshape(b, s, d)
    return x
```

```python
import math

import jax
import jax.numpy as jnp
from jax import lax
from jax.experimental import pallas as pl
from jax.experimental.pallas import tpu as pltpu

D_MODEL = 1024
D_FF = 2816
HEAD_DIM = 64
MEM_LEN = 256
MEM_HEADS = 4
MEM_WIDTH = MEM_HEADS * HEAD_DIM
TOK_WIDTH = D_MODEL - MEM_WIDTH
TOK_HEADS = TOK_WIDTH // HEAD_DIM
HEAD_PAIRS = TOK_HEADS // 2
CHUNK = 128
EPS = 1e-6

LANES = 128
MXU_WIDTH = 256
SUBLANES = 8
PAIR_WIDTH = 2 * HEAD_DIM
NEG_BIG = -1e30
QK_SCALE = 1.0 / math.sqrt(HEAD_DIM)
LOG2E = math.log2(math.e)
SPLIT = 3

FFN_TM = 1024
FFN_FC = 256
MIX_TM = 512
ATT_TK = 512
ATT_TQ = 2 * ATT_TK
VMEM_LIMIT = 56 * 1024 * 1024

KEY_BLOCK = 2 * PAIR_WIDTH
ONES_LANE0 = 2 * SPLIT
VT_BLOCK = 128

F32 = jnp.float32
BF16 = jnp.bfloat16


def _dot(a, b):
    return jnp.dot(a, b, preferred_element_type=F32)


def _dot_nt(a, b):
    return lax.dot_general(a, b, (((1,), (1,)), ((), ())), preferred_element_type=F32)


def _split3(y):
    hi = y.astype(BF16)
    r = y - hi.astype(F32)
    mid = r.astype(BF16)
    lo = (r - mid.astype(F32)).astype(BF16)
    return hi, mid, lo


def _rms_rows(x, gain):
    ms = jnp.mean(x * x, axis=-1, keepdims=True)
    return x * lax.rsqrt(ms + EPS) * gain


def _group_rms(y, p_blk, gain):
    sq = (y * y).astype(BF16)
    ss = jnp.concatenate(
        [_dot(sq[:, c:c + MXU_WIDTH], p_blk) for c in range(0, y.shape[1], MXU_WIDTH)], axis=1)
    return y * lax.rsqrt(ss * (1.0 / HEAD_DIM) + EPS) * gain


def _group_block():
    grp = jnp.arange(MXU_WIDTH) // HEAD_DIM
    return (grp[:, None] == grp[None, :]).astype(BF16)


def _const_spec(shape):
    nd = len(shape)
    return pl.BlockSpec(shape, lambda *_: (0,) * nd, pipeline_mode=pl.Buffered(1))


def _params(sem):
    return pltpu.CompilerParams(dimension_semantics=sem, vmem_limit_bytes=VMEM_LIMIT)


def _ffn_kernel(x_ref, g_ref, wa_ref, wb_ref, wo_ref, o_ref, h_scr, g_scr):
    h_scr[...] = _rms_rows(x_ref[...], g_ref[...]).astype(BF16)
    for c in range(D_FF // FFN_FC):
        cols = slice(c * FFN_FC, (c + 1) * FFN_FC)
        h = h_scr[...]
        a = _dot(h, wa_ref[:, cols])
        b = _dot(h, wb_ref[:, cols])
        g_scr[:, cols] = (a * jax.nn.sigmoid(a) * b).astype(BF16)
    o_ref[...] = x_ref[...] + 0.5 * _dot(g_scr[...], wo_ref[...])


def _ffn(x2, gain, w_in, w_out):
    n = x2.shape[0]
    wa = w_in[:, :D_FF].astype(BF16)
    wb = w_in[:, D_FF:].astype(BF16)
    wo = w_out.astype(BF16)
    return pl.pallas_call(
        _ffn_kernel,
        grid=(n // FFN_TM,),
        in_specs=[
            pl.BlockSpec((FFN_TM, D_MODEL), lambda i: (i, 0)),
            _const_spec((1, D_MODEL)),
            _const_spec((D_MODEL, D_FF)),
            _const_spec((D_MODEL, D_FF)),
            _const_spec((D_FF, D_MODEL)),
        ],
        out_specs=pl.BlockSpec((FFN_TM, D_MODEL), lambda i: (i, 0)),
        out_shape=jax.ShapeDtypeStruct((n, D_MODEL), F32),
        scratch_shapes=[pltpu.VMEM((FFN_TM, D_MODEL), BF16),
                        pltpu.VMEM((FFN_TM, D_FF), BF16)],
        compiler_params=_params(("arbitrary",)),
        name="ffn",
    )(x2, gain.reshape(1, D_MODEL), wa, wb, wo)


def _fox_in_kernel(x_ref, g_ref, w_ref, wvt_ref, bf_ref, gq_ref, gk_ref, gmq_ref,
                   gb_ref, tri_ref, route_ref, ones_ref,
                   q_ref, k_ref, vt_ref, mq_ref, carry_scr):
    @pl.when(pl.program_id(1) == 0)
    def _():
        carry_scr[...] = jnp.zeros_like(carry_scr)

    t = TOK_WIDTH
    h = _rms_rows(x_ref[0], g_ref[...]).astype(BF16)
    f = _dot(h, w_ref[:, 2 * t + MEM_WIDTH:]) + bf_ref[...]
    q = _dot(h, w_ref[:, 0:t])
    log_f = (jnp.minimum(f, 0.0) - jnp.log1p(jnp.exp(-jnp.abs(f)))) * LOG2E
    k = _dot(h, w_ref[:, t:2 * t])
    tri = tri_ref[...]
    c3 = _dot(tri, jnp.concatenate(_split3(log_f), axis=1))
    c = sum(c3[:, j * LANES:(j + 1) * LANES] for j in range(SPLIT)) + carry_scr[0:1, :]
    carry_scr[...] = jnp.broadcast_to(c[MIX_TM - 1:MIX_TM, :], carry_scr.shape)
    vt = _dot_nt(wvt_ref[...], h)
    mq = _dot(h, w_ref[:, 2 * t:2 * t + MEM_WIDTH])
    q_ref[0] = (_group_rms(q, gb_ref[...], gq_ref[...])
                * (QK_SCALE * LOG2E)).astype(BF16)
    decay = _dot(jnp.concatenate(_split3(c), axis=1), route_ref[...]) + ones_ref[...]
    kn = _group_rms(k, gb_ref[...], gk_ref[...])
    mq_ref[0] = (_group_rms(mq, gb_ref[...], gmq_ref[...]) * QK_SCALE).astype(BF16)

    for p in range(HEAD_PAIRS):
        pair = slice(p * PAIR_WIDTH, (p + 1) * PAIR_WIDTH)
        k_ref[0, :, p * KEY_BLOCK:p * KEY_BLOCK + PAIR_WIDTH] = kn[:, pair].astype(BF16)
        k_ref[0, :, p * KEY_BLOCK + PAIR_WIDTH:(p + 1) * KEY_BLOCK] = decay[:, pair].astype(BF16)
    ones_rows = jnp.where(
        lax.broadcasted_iota(jnp.int32, (VT_BLOCK - HEAD_DIM, MIX_TM), 0) == 0, 1.0, 0.0)
    for hd in range(TOK_HEADS):
        vt_ref[0, hd * VT_BLOCK:hd * VT_BLOCK + HEAD_DIM, :] = (
            vt[hd * HEAD_DIM:(hd + 1) * HEAD_DIM, :].astype(BF16))
        vt_ref[0, hd * VT_BLOCK + HEAD_DIM:(hd + 1) * VT_BLOCK, :] = ones_rows.astype(BF16)


def _decay_route():
    r = jnp.zeros((SPLIT * LANES, HEAD_PAIRS * PAIR_WIDTH), F32)
    for hd in range(TOK_HEADS):
        for j in range(SPLIT):
            r = r.at[j * LANES + hd, (hd // 2) * PAIR_WIDTH + (hd % 2) * SPLIT + j].set(-1.0)
    ones = jnp.zeros((1, HEAD_PAIRS * PAIR_WIDTH), F32)
    for p in range(HEAD_PAIRS):
        ones = ones.at[0, p * PAIR_WIDTH + ONES_LANE0:p * PAIR_WIDTH + ONES_LANE0 + SPLIT].set(1.0)
    return r.astype(BF16), ones


def _fox_in(x, gain, w_in, b_f, g_q, g_k, g_mq):
    b, s, _ = x.shape
    t = TOK_WIDTH
    n_gate = 3 * t
    w_all = jnp.concatenate(
        [w_in[:, :2 * t], w_in[:, n_gate + TOK_HEADS:], w_in[:, n_gate:n_gate + TOK_HEADS],
         jnp.zeros((D_MODEL, LANES - TOK_HEADS), w_in.dtype)], axis=1).astype(BF16)
    n_all = w_all.shape[1]
    wvt = w_in[:, 2 * t:n_gate].T.astype(BF16)
    bf = jnp.zeros((1, LANES), F32).at[0, :TOK_HEADS].set(b_f)
    idx = jnp.arange(MIX_TM)
    tri = (idx[None, :] <= idx[:, None]).astype(BF16)
    route, ones = _decay_route()
    tile = lambda w: pl.BlockSpec((1, MIX_TM, w), lambda bi, i: (bi, i, 0))
    return pl.pallas_call(
        _fox_in_kernel,
        grid=(b, s // MIX_TM),
        in_specs=[
            tile(D_MODEL),
            _const_spec((1, D_MODEL)),
            _const_spec((D_MODEL, n_all)),
            _const_spec((t, D_MODEL)),
            _const_spec((1, LANES)),
            _const_spec((1, t)),
            _const_spec((1, t)),
            _const_spec((1, MEM_WIDTH)),
            _const_spec((MXU_WIDTH, MXU_WIDTH)),
            _const_spec((MIX_TM, MIX_TM)),
            _const_spec(route.shape),
            _const_spec(ones.shape),
        ],
        out_specs=[tile(t), tile(HEAD_PAIRS * KEY_BLOCK),
                   pl.BlockSpec((1, TOK_HEADS * VT_BLOCK, MIX_TM), lambda bi, i: (bi, 0, i)),
                   tile(MEM_WIDTH)],
        out_shape=[jax.ShapeDtypeStruct((b, s, t), BF16),
                   jax.ShapeDtypeStruct((b, s, HEAD_PAIRS * KEY_BLOCK), BF16),
                   jax.ShapeDtypeStruct((b, TOK_HEADS * VT_BLOCK, s), BF16),
                   jax.ShapeDtypeStruct((b, s, MEM_WIDTH), BF16)],
        scratch_shapes=[pltpu.VMEM((SUBLANES, LANES), F32)],
        compiler_params=_params(("arbitrary", "arbitrary")),
        name="fox_in",
    )(x, gain.reshape(1, D_MODEL), w_all, wvt, bf,
      jnp.tile(g_q, TOK_HEADS).reshape(1, t), jnp.tile(g_k, TOK_HEADS).reshape(1, t),
      jnp.tile(g_mq, MEM_HEADS).reshape(1, MEM_WIDTH), _group_block(), tri, route, ones)


def _fox_attn_kernel(q_ref, k_ref, vt_ref, o_ref,
                     qa_scr, sa_scr, sb_scr, ta_scr, tb_scr, m_scr, acc_scr):
    def tile(i, carry):
        q0 = pl.multiple_of(i * ATT_TQ, ATT_TQ)
        lane = lax.broadcasted_iota(jnp.int32, (ATT_TQ, PAIR_WIDTH), 1)
        first = lane < HEAD_DIM
        q = q_ref[0, pl.ds(q0, ATT_TQ), :].astype(F32)
        zero = jnp.zeros_like(q)
        base = -k_ref[0, pl.ds(q0, 2 * SUBLANES), PAIR_WIDTH:KEY_BLOCK].astype(F32)
        base = jnp.broadcast_to(base[0:1, :], (ATT_TQ, PAIR_WIDTH))
        in_ones = (lane >= ONES_LANE0) & (lane < ONES_LANE0 + SPLIT)
        for hh in range(2):
            own = (lane >= hh * SPLIT) & (lane < (hh + 1) * SPLIT)
            moved = pltpu.roll(base, ONES_LANE0 - hh * SPLIT, 1)
            qa = jnp.where(own, 1.0, jnp.where(in_ones, moved, 0.0))
            qa_scr[hh, 0:PAIR_WIDTH, :] = jnp.where(first == (hh == 0), q, zero).T.astype(BF16)
            qa_scr[hh, PAIR_WIDTH:KEY_BLOCK, :] = qa.T.astype(BF16)
        m_scr[...] = jnp.full_like(m_scr, NEG_BIG)
        acc_scr[...] = jnp.zeros_like(acc_scr)

        def scores(blk, buf, lo=0):
            s_scr, top_scr = buf
            kb = k_ref[0, pl.ds(pl.multiple_of(blk * ATT_TK, ATT_TK), ATT_TK), :]
            for hh in range(2):
                s = _dot(kb, qa_scr[hh, :, lo:])
                s_scr[hh, :, lo:] = s
                top_scr[hh, :, lo:] = jnp.broadcast_to(
                    jnp.max(s, axis=0, keepdims=True), (SUBLANES, ATT_TQ - lo))

        def consume(blk, buf, masked, lo=0):
            s_scr, top_scr = buf
            k0 = pl.multiple_of(blk * ATT_TK, ATT_TK)
            probs = []
            for hh in range(2):
                s = s_scr[hh, :, lo:]
                if masked:
                    key = lax.broadcasted_iota(jnp.int32, s.shape, 0)
                    qry = lax.broadcasted_iota(jnp.int32, s.shape, 1)
                    s = jnp.where(key <= qry, s, NEG_BIG)
                    top = jnp.max(s, axis=0, keepdims=True)
                else:
                    top = top_scr[hh, 0:1, lo:]
                m_prev = m_scr[hh, 0:1, lo:]
                m_new = jnp.maximum(m_prev, top)
                m_scr[hh, :, lo:] = jnp.broadcast_to(m_new, (SUBLANES, ATT_TQ - lo))
                probs.append((jnp.exp2(s - m_new).astype(BF16), jnp.exp2(m_prev - m_new)))
            for hh in range(2):
                p, alpha = probs[hh]
                vt = vt_ref[0, hh * VT_BLOCK:(hh + 1) * VT_BLOCK, pl.ds(k0, ATT_TK)]
                acc_scr[hh, :, lo:] = alpha * acc_scr[hh, :, lo:] + _dot(vt, p)

        buf_a = (sa_scr, ta_scr)
        buf_b = (sb_scr, tb_scr)
        scores(0, buf_a)

        def body(jj, inner):
            scores(2 * jj + 1, buf_b)
            consume(2 * jj, buf_a, False)
            scores(2 * jj + 2, buf_a)
            consume(2 * jj + 1, buf_b, False)
            return inner

        lax.fori_loop(0, i, body, 0)
        scores(2 * i + 1, buf_b, ATT_TK)
        consume(2 * i, buf_a, True)
        consume(2 * i + 1, buf_b, True, ATT_TK)

        halves = []
        for hh in range(2):
            acc = acc_scr[hh]
            halves.append(acc[0:HEAD_DIM, :] / acc[HEAD_DIM:HEAD_DIM + 1, :])
        o_ref[0, pl.ds(q0, ATT_TQ), :] = jnp.concatenate(halves, axis=0).T.astype(o_ref.dtype)
        return carry

    lax.fori_loop(0, q_ref.shape[1] // ATT_TQ, tile, 0)


def _fox_attn(q, k, v):
    b, s, t = q.shape
    return pl.pallas_call(
        _fox_attn_kernel,
        grid=(b, HEAD_PAIRS),
        in_specs=[
            pl.BlockSpec((1, s, PAIR_WIDTH), lambda bi, p: (bi, 0, p)),
            pl.BlockSpec((1, s, KEY_BLOCK), lambda bi, p: (bi, 0, p)),
            pl.BlockSpec((1, 2 * VT_BLOCK, s), lambda bi, p: (bi, p, 0)),
        ],
        out_specs=pl.BlockSpec((1, s, PAIR_WIDTH), lambda bi, p: (bi, 0, p)),
        out_shape=jax.ShapeDtypeStruct((b, s, t), BF16),
        scratch_shapes=[pltpu.VMEM((2, KEY_BLOCK, ATT_TQ), BF16),
                        pltpu.VMEM((2, ATT_TK, ATT_TQ), F32),
                        pltpu.VMEM((2, ATT_TK, ATT_TQ), F32),
                        pltpu.VMEM((2, SUBLANES, ATT_TQ), F32),
                        pltpu.VMEM((2, SUBLANES, ATT_TQ), F32),
                        pltpu.VMEM((2, SUBLANES, ATT_TQ), F32),
                        pltpu.VMEM((2, VT_BLOCK, ATT_TQ), F32)],
        compiler_params=_params(("arbitrary", "arbitrary")),
        name="fox_attn",
    )(q, k, v)


def _mem_kv_kernel(mem_ref, g_ref, w_ref, gk_ref, gb_ref, k_ref, v_ref):
    mem_n = _rms_rows(mem_ref[0], g_ref[...]).astype(BF16)
    kv = _dot(mem_n, w_ref[...])
    k_ref[0] = _group_rms(kv[:, :MEM_WIDTH], gb_ref[...], gk_ref[...]).T.astype(BF16)
    v_ref[0] = kv[:, MEM_WIDTH:].astype(BF16)


def _mem_kv(mem, g_mem, w_kv, g_k):
    b = mem.shape[0]
    blk = pl.BlockSpec((1, MEM_LEN, MEM_WIDTH), lambda bi: (bi, 0, 0))
    return pl.pallas_call(
        _mem_kv_kernel,
        grid=(b,),
        in_specs=[
            pl.BlockSpec((1, MEM_LEN, D_MODEL), lambda bi: (bi, 0, 0)),
            _const_spec((1, D_MODEL)),
            _const_spec((D_MODEL, 2 * MEM_WIDTH)),
            _const_spec((1, MEM_WIDTH)),
            _const_spec((MXU_WIDTH, MXU_WIDTH)),
        ],
        out_specs=[blk, blk],
        out_shape=[jax.ShapeDtypeStruct((b, MEM_LEN, MEM_WIDTH), BF16)] * 2,
        compiler_params=_params(("arbitrary",)),
        name="mem_kv",
    )(mem, g_mem.reshape(1, D_MODEL), w_kv.astype(BF16),
      jnp.tile(g_k, MEM_HEADS).reshape(1, MEM_WIDTH), _group_block())


def _mix_out_kernel(x_ref, tok_ref, mq_ref, km_ref, vm_ref, wt_ref, wm_ref, o_ref):
    mq = mq_ref[0]
    km = km_ref[0]
    vm = vm_ref[0]
    lane = lax.broadcasted_iota(jnp.int32, (MIX_TM, MEM_WIDTH), 1) // HEAD_DIM
    zero = jnp.zeros_like(mq)
    scores = [_dot(jnp.where(lane == hh, mq, zero), km) for hh in range(MEM_HEADS)]
    y = _dot(tok_ref[0], wt_ref[...])
    probs = []
    for s in scores:
        p = jnp.exp(s - jnp.max(s, axis=-1, keepdims=True))
        probs.append((p / jnp.sum(p, axis=-1, keepdims=True)).astype(BF16))
    mo = jnp.zeros((MIX_TM, MEM_WIDTH), F32)
    for hh in range(MEM_HEADS):
        mo = jnp.where(lane == hh, _dot(probs[hh], vm), mo)
    o_ref[0] = x_ref[0] + y + _dot(mo.astype(BF16), wm_ref[...])


def _mix_out(x, tok, mq, km, vm, w_out):
    b, s, _ = x.shape
    wt = w_out[:TOK_WIDTH].astype(BF16)
    wm = w_out[TOK_WIDTH:].astype(BF16)
    tile = lambda w: pl.BlockSpec((1, MIX_TM, w), lambda bi, i: (bi, i, 0))
    memblk = pl.BlockSpec((1, MEM_LEN, MEM_WIDTH), lambda bi, i: (bi, 0, 0))
    return pl.pallas_call(
        _mix_out_kernel,
        grid=(b, s // MIX_TM),
        in_specs=[tile(D_MODEL), tile(TOK_WIDTH), tile(MEM_WIDTH), memblk, memblk,
                  _const_spec((TOK_WIDTH, D_MODEL)), _const_spec((MEM_WIDTH, D_MODEL))],
        out_specs=tile(D_MODEL),
        out_shape=jax.ShapeDtypeStruct((b, s, D_MODEL), F32),
        compiler_params=_params(("arbitrary", "arbitrary")),
        name="mix_out",
    )(x, tok, mq, km, vm, wt, wm)


def _gelu_tanh(x):
    return 0.5 * x * (1.0 + jnp.tanh(math.sqrt(2.0 / math.pi) * (x + 0.044715 * (x * x * x))))


def _gmlp_in_kernel(x_ref, g_ref, wuv_ref, wmq_ref, gv_ref, gmq_ref, ws_ref, bs_ref,
                    gb_ref, tok_ref, mq_ref, vn_scr, u_scr):
    h = _rms_rows(x_ref[0], g_ref[...]).astype(BF16)
    t = TOK_WIDTH
    n_chunk = MIX_TM // CHUNK
    v = _dot(h, wuv_ref[:, t:2 * t])
    mq = _dot(h, wmq_ref[...])
    u_scr[...] = _gelu_tanh(_dot(h, wuv_ref[:, 0:t]))
    vn_scr[...] = _group_rms(_gelu_tanh(v), gb_ref[...], gv_ref[...]).astype(BF16)
    mq_ref[0] = (_group_rms(mq, gb_ref[...], gmq_ref[...]) * QK_SCALE).astype(BF16)

    row = lax.broadcasted_iota(jnp.int32, (CHUNK, CHUNK), 0)
    col = lax.broadcasted_iota(jnp.int32, (CHUNK, CHUNK), 1)
    causal = col <= row
    first = lax.broadcasted_iota(jnp.int32, (CHUNK, PAIR_WIDTH), 1) < HEAD_DIM
    for p in range(HEAD_PAIRS):
        lanes = slice(p * PAIR_WIDTH, (p + 1) * PAIR_WIDTH)
        v_cat = jnp.concatenate(
            [vn_scr[c * CHUNK:(c + 1) * CHUNK, lanes] for c in range(n_chunk)], axis=1)
        wa = jnp.where(causal, ws_ref[2 * p], 0.0).astype(BF16)
        wb = jnp.where(causal, ws_ref[2 * p + 1], 0.0).astype(BF16)
        ga = _dot(wa, v_cat)
        gb = _dot(wb, v_cat)
        bias = bs_ref[:, lanes]
        for c in range(n_chunk):
            cols = slice(c * PAIR_WIDTH, (c + 1) * PAIR_WIDTH)
            gate = jnp.where(first, ga[:, cols], gb[:, cols]) + bias
            rows = slice(c * CHUNK, (c + 1) * CHUNK)
            tok_ref[0, rows, lanes] = (u_scr[rows, lanes] * gate).astype(BF16)


def _gmlp_in(x, gain, w_in, v_gain, w_s, b_s, g_mq):
    b, s, _ = x.shape
    t = TOK_WIDTH
    wuv = w_in[:, :2 * t].astype(BF16)
    wmq = w_in[:, 2 * t:].astype(BF16)
    bs_full = jnp.repeat(b_s.T, HEAD_DIM, axis=1)
    tile = lambda w: pl.BlockSpec((1, MIX_TM, w), lambda bi, i: (bi, i, 0))
    return pl.pallas_call(
        _gmlp_in_kernel,
        grid=(b, s // MIX_TM),
        in_specs=[
            tile(D_MODEL),
            _const_spec((1, D_MODEL)),
            _const_spec((D_MODEL, 2 * t)),
            _const_spec((D_MODEL, MEM_WIDTH)),
            _const_spec((1, t)),
            _const_spec((1, MEM_WIDTH)),
            _const_spec((TOK_HEADS, CHUNK, CHUNK)),
            _const_spec((CHUNK, t)),
            _const_spec((MXU_WIDTH, MXU_WIDTH)),
        ],
        out_specs=[tile(t), tile(MEM_WIDTH)],
        out_shape=[jax.ShapeDtypeStruct((b, s, t), BF16),
                   jax.ShapeDtypeStruct((b, s, MEM_WIDTH), BF16)],
        scratch_shapes=[pltpu.VMEM((MIX_TM, t), BF16), pltpu.VMEM((MIX_TM, t), F32)],
        compiler_params=_params(("arbitrary", "arbitrary")),
        name="gmlp_in",
    )(x, gain.reshape(1, D_MODEL), wuv, wmq, v_gain.reshape(1, t),
      jnp.tile(g_mq, MEM_HEADS).reshape(1, MEM_WIDTH), w_s, bs_full, _group_block())


def kernel(x, mem, norm_ffn1, ffn1_w_in, ffn1_w_out, norm_mix, norm_ffn2, ffn2_w_in, ffn2_w_out,
           w_out, mem_norm, mem_w_kv, mem_q_norm, mem_k_norm, fox_w_in, fox_b_f, fox_q_norm,
           fox_k_norm, gmlp_w_in, gmlp_v_norm, gmlp_w_s, gmlp_b_s):
    b, s, d = x.shape
    depth = norm_mix.shape[0]
    for i in range(depth):
        kind, j = i % 2, i // 2
        x = _ffn(x.reshape(b * s, d), norm_ffn1[i], ffn1_w_in[i], ffn1_w_out[i]).reshape(b, s, d)
        km, vm = _mem_kv(mem, mem_norm, mem_w_kv[i], mem_k_norm[i])
        if kind == 0:
            q, k, v, mq = _fox_in(x, norm_mix[i], fox_w_in[j], fox_b_f[j], fox_q_norm[j],
                                  fox_k_norm[j], mem_q_norm[i])
            tok = _fox_attn(q, k, v)
        else:
            tok, mq = _gmlp_in(x, norm_mix[i], gmlp_w_in[j], gmlp_v_norm[j], gmlp_w_s[j],
                               gmlp_b_s[j], mem_q_norm[i])
        x = _mix_out(x, tok, mq, km, vm, w_out[i])
        x = _ffn(x.reshape(b * s, d), norm_ffn2[i], ffn2_w_in[i], ffn2_w_out[i]).reshape(b, s, d)
    return x
```

```python
import math

import jax
import jax.numpy as jnp
from jax import lax
from jax.experimental import pallas as pl
from jax.experimental.pallas import tpu as pltpu

D_MODEL = 1024
D_FF = 2816
HEAD_DIM = 64
MEM_LEN = 256
MEM_HEADS = 4
MEM_WIDTH = MEM_HEADS * HEAD_DIM
TOK_WIDTH = D_MODEL - MEM_WIDTH
TOK_HEADS = TOK_WIDTH // HEAD_DIM
HEAD_PAIRS = TOK_HEADS // 2
CHUNK = 128
EPS = 1e-6

LANES = 128
MXU_WIDTH = 256
SUBLANES = 8
PAIR_WIDTH = 2 * HEAD_DIM
NEG_BIG = -1e30
QK_SCALE = 1.0 / math.sqrt(HEAD_DIM)
LOG2E = math.log2(math.e)
SPLIT = 3

FFN_TM = 1024
FFN_FC = 256
MIX_TM = 512
ATT_TK = 512
ATT_TQ = 2 * ATT_TK
VMEM_LIMIT = 56 * 1024 * 1024

KEY_BLOCK = 2 * PAIR_WIDTH
ONES_LANE0 = 2 * SPLIT
VT_BLOCK = 128

F32 = jnp.float32
BF16 = jnp.bfloat16


def _dot(a, b):
    return jnp.dot(a, b, preferred_element_type=F32)


def _dot_nt(a, b):
    return lax.dot_general(a, b, (((1,), (1,)), ((), ())), preferred_element_type=F32)


def _split3(y):
    hi = y.astype(BF16)
    r = y - hi.astype(F32)
    mid = r.astype(BF16)
    lo = (r - mid.astype(F32)).astype(BF16)
    return hi, mid, lo


def _rms_rows(x, gain):
    ms = jnp.mean(x * x, axis=-1, keepdims=True)
    return x * lax.rsqrt(ms + EPS) * gain


def _group_rms(y, p_blk, gain):
    sq = (y * y).astype(BF16)
    ss = jnp.concatenate(
        [_dot(sq[:, c:c + MXU_WIDTH], p_blk) for c in range(0, y.shape[1], MXU_WIDTH)], axis=1)
    return y * lax.rsqrt(ss * (1.0 / HEAD_DIM) + EPS) * gain


def _group_block():
    grp = jnp.arange(MXU_WIDTH) // HEAD_DIM
    return (grp[:, None] == grp[None, :]).astype(BF16)


def _const_spec(shape):
    nd = len(shape)
    return pl.BlockSpec(shape, lambda *_: (0,) * nd, pipeline_mode=pl.Buffered(1))


def _params(sem):
    return pltpu.CompilerParams(dimension_semantics=sem, vmem_limit_bytes=VMEM_LIMIT)


def _ffn_tile(x_ref, g_ref, wa_ref, wb_ref, wo_ref, h_scr, g_scr):
    h_scr[...] = _rms_rows(x_ref[...], g_ref[...]).astype(BF16)
    for c in range(D_FF // FFN_FC):
        cols = slice(c * FFN_FC, (c + 1) * FFN_FC)
        h = h_scr[...]
        a = _dot(h, wa_ref[:, cols])
        b = _dot(h, wb_ref[:, cols])
        g_scr[:, cols] = (a * jax.nn.sigmoid(a) * b).astype(BF16)
    return x_ref[...] + 0.5 * _dot(g_scr[...], wo_ref[...])


def _ffn_kernel(x_ref, g_ref, wa_ref, wb_ref, wo_ref, o_ref, h_scr, g_scr):
    o_ref[...] = _ffn_tile(x_ref, g_ref, wa_ref, wb_ref, wo_ref, h_scr, g_scr)


def _ffn_weights(w_in, w_out):
    return w_in[:, :D_FF].astype(BF16), w_in[:, D_FF:].astype(BF16), w_out.astype(BF16)


def _ffn(x2, gain, w_in, w_out):
    n = x2.shape[0]
    wa, wb, wo = _ffn_weights(w_in, w_out)
    return pl.pallas_call(
        _ffn_kernel,
        grid=(n // FFN_TM,),
        in_specs=[
            pl.BlockSpec((FFN_TM, D_MODEL), lambda i: (i, 0)),
            _const_spec((1, D_MODEL)),
            _const_spec((D_MODEL, D_FF)),
            _const_spec((D_MODEL, D_FF)),
            _const_spec((D_FF, D_MODEL)),
        ],
        out_specs=pl.BlockSpec((FFN_TM, D_MODEL), lambda i: (i, 0)),
        out_shape=jax.ShapeDtypeStruct((n, D_MODEL), F32),
        scratch_shapes=[pltpu.VMEM((FFN_TM, D_MODEL), BF16),
                        pltpu.VMEM((FFN_TM, D_FF), BF16)],
        compiler_params=_params(("arbitrary",)),
        name="ffn",
    )(x2, gain.reshape(1, D_MODEL), wa, wb, wo)


def _fox_in_kernel(x_ref, g_ref, w_ref, wvt_ref, bf_ref, gq_ref, gk_ref, gmq_ref,
                   gb_ref, tri_ref, route_ref, ones_ref,
                   q_ref, k_ref, vt_ref, mq_ref, carry_scr):
    @pl.when(pl.program_id(1) == 0)
    def _():
        carry_scr[...] = jnp.zeros_like(carry_scr)

    t = TOK_WIDTH
    h = _rms_rows(x_ref[0], g_ref[...]).astype(BF16)
    f = _dot(h, w_ref[:, 2 * t + MEM_WIDTH:]) + bf_ref[...]
    q = _dot(h, w_ref[:, 0:t])
    log_f = (jnp.minimum(f, 0.0) - jnp.log1p(jnp.exp(-jnp.abs(f)))) * LOG2E
    k = _dot(h, w_ref[:, t:2 * t])
    tri = tri_ref[...]
    c3 = _dot(tri, jnp.concatenate(_split3(log_f), axis=1))
    c = sum(c3[:, j * LANES:(j + 1) * LANES] for j in range(SPLIT)) + carry_scr[0:1, :]
    carry_scr[...] = jnp.broadcast_to(c[MIX_TM - 1:MIX_TM, :], carry_scr.shape)
    vt = _dot_nt(wvt_ref[...], h)
    mq = _dot(h, w_ref[:, 2 * t:2 * t + MEM_WIDTH])
    q_ref[0] = (_group_rms(q, gb_ref[...], gq_ref[...])
                * (QK_SCALE * LOG2E)).astype(BF16)
    decay = _dot(jnp.concatenate(_split3(c), axis=1), route_ref[...]) + ones_ref[...]
    kn = _group_rms(k, gb_ref[...], gk_ref[...])
    mq_ref[0] = (_group_rms(mq, gb_ref[...], gmq_ref[...]) * QK_SCALE).astype(BF16)

    for p in range(HEAD_PAIRS):
        pair = slice(p * PAIR_WIDTH, (p + 1) * PAIR_WIDTH)
        k_ref[0, :, p * KEY_BLOCK:p * KEY_BLOCK + PAIR_WIDTH] = kn[:, pair].astype(BF16)
        k_ref[0, :, p * KEY_BLOCK + PAIR_WIDTH:(p + 1) * KEY_BLOCK] = decay[:, pair].astype(BF16)
    ones_rows = jnp.where(
        lax.broadcasted_iota(jnp.int32, (VT_BLOCK - HEAD_DIM, MIX_TM), 0) == 0, 1.0, 0.0)
    for hd in range(TOK_HEADS):
        vt_ref[0, hd * VT_BLOCK:hd * VT_BLOCK + HEAD_DIM, :] = (
            vt[hd * HEAD_DIM:(hd + 1) * HEAD_DIM, :].astype(BF16))
        vt_ref[0, hd * VT_BLOCK + HEAD_DIM:(hd + 1) * VT_BLOCK, :] = ones_rows.astype(BF16)


def _decay_route():
    r = jnp.zeros((SPLIT * LANES, HEAD_PAIRS * PAIR_WIDTH), F32)
    for hd in range(TOK_HEADS):
        for j in range(SPLIT):
            r = r.at[j * LANES + hd, (hd // 2) * PAIR_WIDTH + (hd % 2) * SPLIT + j].set(-1.0)
    ones = jnp.zeros((1, HEAD_PAIRS * PAIR_WIDTH), F32)
    for p in range(HEAD_PAIRS):
        ones = ones.at[0, p * PAIR_WIDTH + ONES_LANE0:p * PAIR_WIDTH + ONES_LANE0 + SPLIT].set(1.0)
    return r.astype(BF16), ones


def _fox_in(x, gain, w_in, b_f, g_q, g_k, g_mq):
    b, s, _ = x.shape
    t = TOK_WIDTH
    n_gate = 3 * t
    w_all = jnp.concatenate(
        [w_in[:, :2 * t], w_in[:, n_gate + TOK_HEADS:], w_in[:, n_gate:n_gate + TOK_HEADS],
         jnp.zeros((D_MODEL, LANES - TOK_HEADS), w_in.dtype)], axis=1).astype(BF16)
    n_all = w_all.shape[1]
    wvt = w_in[:, 2 * t:n_gate].T.astype(BF16)
    bf = jnp.zeros((1, LANES), F32).at[0, :TOK_HEADS].set(b_f)
    idx = jnp.arange(MIX_TM)
    tri = (idx[None, :] <= idx[:, None]).astype(BF16)
    route, ones = _decay_route()
    tile = lambda w: pl.BlockSpec((1, MIX_TM, w), lambda bi, i: (bi, i, 0))
    return pl.pallas_call(
        _fox_in_kernel,
        grid=(b, s // MIX_TM),
        in_specs=[
            tile(D_MODEL),
            _const_spec((1, D_MODEL)),
            _const_spec((D_MODEL, n_all)),
            _const_spec((t, D_MODEL)),
            _const_spec((1, LANES)),
            _const_spec((1, t)),
            _const_spec((1, t)),
            _const_spec((1, MEM_WIDTH)),
            _const_spec((MXU_WIDTH, MXU_WIDTH)),
            _const_spec((MIX_TM, MIX_TM)),
            _const_spec(route.shape),
            _const_spec(ones.shape),
        ],
        out_specs=[tile(t), tile(HEAD_PAIRS * KEY_BLOCK),
                   pl.BlockSpec((1, TOK_HEADS * VT_BLOCK, MIX_TM), lambda bi, i: (bi, 0, i)),
                   tile(MEM_WIDTH)],
        out_shape=[jax.ShapeDtypeStruct((b, s, t), BF16),
                   jax.ShapeDtypeStruct((b, s, HEAD_PAIRS * KEY_BLOCK), BF16),
                   jax.ShapeDtypeStruct((b, TOK_HEADS * VT_BLOCK, s), BF16),
                   jax.ShapeDtypeStruct((b, s, MEM_WIDTH), BF16)],
        scratch_shapes=[pltpu.VMEM((SUBLANES, LANES), F32)],
        compiler_params=_params(("arbitrary", "arbitrary")),
        name="fox_in",
    )(x, gain.reshape(1, D_MODEL), w_all, wvt, bf,
      jnp.tile(g_q, TOK_HEADS).reshape(1, t), jnp.tile(g_k, TOK_HEADS).reshape(1, t),
      jnp.tile(g_mq, MEM_HEADS).reshape(1, MEM_WIDTH), _group_block(), tri, route, ones)


def _fox_attn_kernel(q_ref, k_ref, vt_ref, o_ref,
                     qa_scr, sa_scr, sb_scr, ta_scr, tb_scr, m_scr, acc_scr):
    n_tiles = q_ref.shape[1] // ATT_TQ
    buf_a = (sa_scr, ta_scr)
    buf_b = (sb_scr, tb_scr)

    def prep(t, slot):
        q0 = pl.multiple_of(t * ATT_TQ, ATT_TQ)
        q_t = q_ref[0, pl.ds(q0, ATT_TQ), :].astype(F32).T
        row = lax.broadcasted_iota(jnp.int32, (PAIR_WIDTH, ATT_TQ), 0)
        lane = lax.broadcasted_iota(jnp.int32, (PAIR_WIDTH, PAIR_WIDTH), 1)
        base = -k_ref[0, pl.ds(q0, 2 * SUBLANES), PAIR_WIDTH:KEY_BLOCK].astype(F32)
        base = jnp.broadcast_to(base[0:1, :], (PAIR_WIDTH, PAIR_WIDTH))
        in_ones = (lane >= ONES_LANE0) & (lane < ONES_LANE0 + SPLIT)
        for hh in range(2):
            own = (lane >= hh * SPLIT) & (lane < (hh + 1) * SPLIT)
            moved = pltpu.roll(base, ONES_LANE0 - hh * SPLIT, 1)
            qa_t = jnp.where(own, 1.0, jnp.where(in_ones, moved, 0.0)).T
            qa_scr[slot, hh, 0:PAIR_WIDTH, :] = jnp.where(
                (row < HEAD_DIM) == (hh == 0), q_t, 0.0).astype(BF16)
            qa_scr[slot, hh, PAIR_WIDTH:KEY_BLOCK, :] = jnp.concatenate(
                [qa_t] * (ATT_TQ // PAIR_WIDTH), axis=1).astype(BF16)

    def scores(slot, blk, buf, lo=0):
        s_scr, top_scr = buf
        kb = k_ref[0, pl.ds(pl.multiple_of(blk * ATT_TK, ATT_TK), ATT_TK), :]
        for hh in range(2):
            s = _dot(kb, qa_scr[slot, hh, :, lo:])
            s_scr[hh, :, lo:] = s
            top_scr[hh, :, lo:] = jnp.broadcast_to(
                jnp.max(s, axis=0, keepdims=True), (SUBLANES, ATT_TQ - lo))

    def consume(blk, buf, masked, lo=0):
        s_scr, top_scr = buf
        k0 = pl.multiple_of(blk * ATT_TK, ATT_TK)
        probs = []
        for hh in range(2):
            s = s_scr[hh, :, lo:]
            if masked:
                key = lax.broadcasted_iota(jnp.int32, s.shape, 0)
                qry = lax.broadcasted_iota(jnp.int32, s.shape, 1)
                s = jnp.where(key <= qry, s, NEG_BIG)
                top = jnp.max(s, axis=0, keepdims=True)
            else:
                top = top_scr[hh, 0:1, lo:]
            m_prev = m_scr[hh, 0:1, lo:]
            m_new = jnp.maximum(m_prev, top)
            m_scr[hh, :, lo:] = jnp.broadcast_to(m_new, (SUBLANES, ATT_TQ - lo))
            probs.append((jnp.exp2(s - m_new).astype(BF16), jnp.exp2(m_prev - m_new)))
        for hh in range(2):
            p, alpha = probs[hh]
            vt = vt_ref[0, hh * VT_BLOCK:(hh + 1) * VT_BLOCK, pl.ds(k0, ATT_TK)]
            acc_scr[hh, :, lo:] = alpha * acc_scr[hh, :, lo:] + _dot(vt, p)

    def tile(i, carry):
        slot = i % 2
        m_scr[...] = jnp.full_like(m_scr, NEG_BIG)
        acc_scr[...] = jnp.zeros_like(acc_scr)

        def body(jj, inner):
            scores(slot, 2 * jj + 1, buf_b)
            consume(2 * jj, buf_a, False)
            scores(slot, 2 * jj + 2, buf_a)
            consume(2 * jj + 1, buf_b, False)
            return inner

        lax.fori_loop(0, i, body, 0)
        scores(slot, 2 * i + 1, buf_b, ATT_TK)
        prep(jnp.minimum(i + 1, n_tiles - 1), 1 - slot)
        consume(2 * i, buf_a, True)
        scores(1 - slot, 0, buf_a)
        consume(2 * i + 1, buf_b, True, ATT_TK)

        halves = []
        for hh in range(2):
            acc = acc_scr[hh]
            halves.append(acc[0:HEAD_DIM, :] / acc[HEAD_DIM:HEAD_DIM + 1, :])
        q0 = pl.multiple_of(i * ATT_TQ, ATT_TQ)
        o_ref[0, pl.ds(q0, ATT_TQ), :] = jnp.concatenate(halves, axis=0).T.astype(o_ref.dtype)
        return carry

    prep(0, 0)
    scores(0, 0, buf_a)
    lax.fori_loop(0, n_tiles, tile, 0)


def _fox_attn(q, k, v):
    b, s, t = q.shape
    return pl.pallas_call(
        _fox_attn_kernel,
        grid=(b, HEAD_PAIRS),
        in_specs=[
            pl.BlockSpec((1, s, PAIR_WIDTH), lambda bi, p: (bi, 0, p)),
            pl.BlockSpec((1, s, KEY_BLOCK), lambda bi, p: (bi, 0, p)),
            pl.BlockSpec((1, 2 * VT_BLOCK, s), lambda bi, p: (bi, p, 0)),
        ],
        out_specs=pl.BlockSpec((1, s, PAIR_WIDTH), lambda bi, p: (bi, 0, p)),
        out_shape=jax.ShapeDtypeStruct((b, s, t), BF16),
        scratch_shapes=[pltpu.VMEM((2, 2, KEY_BLOCK, ATT_TQ), BF16),
                        pltpu.VMEM((2, ATT_TK, ATT_TQ), F32),
                        pltpu.VMEM((2, ATT_TK, ATT_TQ), F32),
                        pltpu.VMEM((2, SUBLANES, ATT_TQ), F32),
                        pltpu.VMEM((2, SUBLANES, ATT_TQ), F32),
                        pltpu.VMEM((2, SUBLANES, ATT_TQ), F32),
                        pltpu.VMEM((2, VT_BLOCK, ATT_TQ), F32)],
        compiler_params=_params(("arbitrary", "arbitrary")),
        name="fox_attn",
    )(q, k, v)


def _mem_kv_kernel(mem_ref, g_ref, w_ref, gk_ref, gb_ref, k_ref, v_ref):
    mem_n = _rms_rows(mem_ref[0], g_ref[...]).astype(BF16)
    kv = _dot(mem_n, w_ref[...])
    k_ref[0] = _group_rms(kv[:, :MEM_WIDTH], gb_ref[...], gk_ref[...]).T.astype(BF16)
    v_ref[0] = kv[:, MEM_WIDTH:].astype(BF16)


def _mem_kv(mem, g_mem, w_kv, g_k):
    b = mem.shape[0]
    blk = pl.BlockSpec((1, MEM_LEN, MEM_WIDTH), lambda bi: (bi, 0, 0))
    return pl.pallas_call(
        _mem_kv_kernel,
        grid=(b,),
        in_specs=[
            pl.BlockSpec((1, MEM_LEN, D_MODEL), lambda bi: (bi, 0, 0)),
            _const_spec((1, D_MODEL)),
            _const_spec((D_MODEL, 2 * MEM_WIDTH)),
            _const_spec((1, MEM_WIDTH)),
            _const_spec((MXU_WIDTH, MXU_WIDTH)),
        ],
        out_specs=[blk, blk],
        out_shape=[jax.ShapeDtypeStruct((b, MEM_LEN, MEM_WIDTH), BF16)] * 2,
        compiler_params=_params(("arbitrary",)),
        name="mem_kv",
    )(mem, g_mem.reshape(1, D_MODEL), w_kv.astype(BF16),
      jnp.tile(g_k, MEM_HEADS).reshape(1, MEM_WIDTH), _group_block())


def _mix_ffn_kernel(x_ref, tok_ref, mq_ref, km_ref, vm_ref, wt_ref, wm_ref,
                    g_ref, wa_ref, wb_ref, wo_ref, o_ref, x1_scr, h_scr, g_scr):
    mq = mq_ref[0]
    km = km_ref[0]
    vm = vm_ref[0]
    lane = lax.broadcasted_iota(jnp.int32, (MIX_TM, MEM_WIDTH), 1) // HEAD_DIM
    zero = jnp.zeros_like(mq)
    scores = [_dot(jnp.where(lane == hh, mq, zero), km) for hh in range(MEM_HEADS)]
    y = _dot(tok_ref[0], wt_ref[...])
    probs = []
    for s in scores:
        p = jnp.exp(s - jnp.max(s, axis=-1, keepdims=True))
        probs.append((p / jnp.sum(p, axis=-1, keepdims=True)).astype(BF16))
    mo = jnp.zeros((MIX_TM, MEM_WIDTH), F32)
    for hh in range(MEM_HEADS):
        mo = jnp.where(lane == hh, _dot(probs[hh], vm), mo)
    x1_scr[...] = x_ref[0] + y + _dot(mo.astype(BF16), wm_ref[...])
    o_ref[0] = _ffn_tile(x1_scr, g_ref, wa_ref, wb_ref, wo_ref, h_scr, g_scr)


def _mix_ffn(x, tok, mq, km, vm, w_out, gain, w_in, w_ffn_out):
    b, s, _ = x.shape
    wt = w_out[:TOK_WIDTH].astype(BF16)
    wm = w_out[TOK_WIDTH:].astype(BF16)
    wa, wb, wo = _ffn_weights(w_in, w_ffn_out)
    tile = lambda w: pl.BlockSpec((1, MIX_TM, w), lambda bi, i: (bi, i, 0))
    memblk = pl.BlockSpec((1, MEM_LEN, MEM_WIDTH), lambda bi, i: (bi, 0, 0))
    return pl.pallas_call(
        _mix_ffn_kernel,
        grid=(b, s // MIX_TM),
        in_specs=[tile(D_MODEL), tile(TOK_WIDTH), tile(MEM_WIDTH), memblk, memblk,
                  _const_spec((TOK_WIDTH, D_MODEL)), _const_spec((MEM_WIDTH, D_MODEL)),
                  _const_spec((1, D_MODEL)), _const_spec((D_MODEL, D_FF)),
                  _const_spec((D_MODEL, D_FF)), _const_spec((D_FF, D_MODEL))],
        out_specs=tile(D_MODEL),
        out_shape=jax.ShapeDtypeStruct((b, s, D_MODEL), F32),
        scratch_shapes=[pltpu.VMEM((MIX_TM, D_MODEL), F32),
                        pltpu.VMEM((MIX_TM, D_MODEL), BF16),
                        pltpu.VMEM((MIX_TM, D_FF), BF16)],
        compiler_params=_params(("arbitrary", "arbitrary")),
        name="mix_ffn",
    )(x, tok, mq, km, vm, wt, wm, gain.reshape(1, D_MODEL), wa, wb, wo)


def _gelu_tanh(x):
    return 0.5 * x * (1.0 + jnp.tanh(math.sqrt(2.0 / math.pi) * (x + 0.044715 * (x * x * x))))


def _gmlp_in_kernel(x_ref, g_ref, wuv_ref, wmq_ref, gv_ref, gmq_ref, ws_ref, bs_ref,
                    gb_ref, tok_ref, mq_ref, vn_scr, u_scr):
    h = _rms_rows(x_ref[0], g_ref[...]).astype(BF16)
    t = TOK_WIDTH
    n_chunk = MIX_TM // CHUNK
    v = _dot(h, wuv_ref[:, t:2 * t])
    mq = _dot(h, wmq_ref[...])
    u_scr[...] = _gelu_tanh(_dot(h, wuv_ref[:, 0:t]))
    vn_scr[...] = _group_rms(_gelu_tanh(v), gb_ref[...], gv_ref[...]).astype(BF16)
    mq_ref[0] = (_group_rms(mq, gb_ref[...], gmq_ref[...]) * QK_SCALE).astype(BF16)

    row = lax.broadcasted_iota(jnp.int32, (CHUNK, CHUNK), 0)
    col = lax.broadcasted_iota(jnp.int32, (CHUNK, CHUNK), 1)
    causal = col <= row
    first = lax.broadcasted_iota(jnp.int32, (CHUNK, PAIR_WIDTH), 1) < HEAD_DIM
    for p in range(HEAD_PAIRS):
        lanes = slice(p * PAIR_WIDTH, (p + 1) * PAIR_WIDTH)
        v_cat = jnp.concatenate(
            [vn_scr[c * CHUNK:(c + 1) * CHUNK, lanes] for c in range(n_chunk)], axis=1)
        wa = jnp.where(causal, ws_ref[2 * p], 0.0).astype(BF16)
        wb = jnp.where(causal, ws_ref[2 * p + 1], 0.0).astype(BF16)
        ga = _dot(wa, v_cat)
        gb = _dot(wb, v_cat)
        bias = bs_ref[:, lanes]
        for c in range(n_chunk):
            cols = slice(c * PAIR_WIDTH, (c + 1) * PAIR_WIDTH)
            gate = jnp.where(first, ga[:, cols], gb[:, cols]) + bias
            rows = slice(c * CHUNK, (c + 1) * CHUNK)
            tok_ref[0, rows, lanes] = (u_scr[rows, lanes] * gate).astype(BF16)


def _gmlp_in(x, gain, w_in, v_gain, w_s, b_s, g_mq):
    b, s, _ = x.shape
    t = TOK_WIDTH
    wuv = w_in[:, :2 * t].astype(BF16)
    wmq = w_in[:, 2 * t:].astype(BF16)
    bs_full = jnp.repeat(b_s.T, HEAD_DIM, axis=1)
    tile = lambda w: pl.BlockSpec((1, MIX_TM, w), lambda bi, i: (bi, i, 0))
    return pl.pallas_call(
        _gmlp_in_kernel,
        grid=(b, s // MIX_TM),
        in_specs=[
            tile(D_MODEL),
            _const_spec((1, D_MODEL)),
            _const_spec((D_MODEL, 2 * t)),
            _const_spec((D_MODEL, MEM_WIDTH)),
            _const_spec((1, t)),
            _const_spec((1, MEM_WIDTH)),
            _const_spec((TOK_HEADS, CHUNK, CHUNK)),
            _const_spec((CHUNK, t)),
            _const_spec((MXU_WIDTH, MXU_WIDTH)),
        ],
        out_specs=[tile(t), tile(MEM_WIDTH)],
        out_shape=[jax.ShapeDtypeStruct((b, s, t), BF16),
                   jax.ShapeDtypeStruct((b, s, MEM_WIDTH), BF16)],
        scratch_shapes=[pltpu.VMEM((MIX_TM, t), BF16), pltpu.VMEM((MIX_TM, t), F32)],
        compiler_params=_params(("arbitrary", "arbitrary")),
        name="gmlp_in",
    )(x, gain.reshape(1, D_MODEL), wuv, wmq, v_gain.reshape(1, t),
      jnp.tile(g_mq, MEM_HEADS).reshape(1, MEM_WIDTH), w_s, bs_full, _group_block())


def kernel(x, mem, norm_ffn1, ffn1_w_in, ffn1_w_out, norm_mix, norm_ffn2, ffn2_w_in, ffn2_w_out,
           w_out, mem_norm, mem_w_kv, mem_q_norm, mem_k_norm, fox_w_in, fox_b_f, fox_q_norm,
           fox_k_norm, gmlp_w_in, gmlp_v_norm, gmlp_w_s, gmlp_b_s):
    b, s, d = x.shape
    depth = norm_mix.shape[0]
    for i in range(depth):
        kind, j = i % 2, i // 2
        x = _ffn(x.reshape(b * s, d), norm_ffn1[i], ffn1_w_in[i], ffn1_w_out[i]).reshape(b, s, d)
        km, vm = _mem_kv(mem, mem_norm, mem_w_kv[i], mem_k_norm[i])
        if kind == 0:
            q, k, v, mq = _fox_in(x, norm_mix[i], fox_w_in[j], fox_b_f[j], fox_q_norm[j],
                                  fox_k_norm[j], mem_q_norm[i])
            tok = _fox_attn(q, k, v)
        else:
            tok, mq = _gmlp_in(x, norm_mix[i], gmlp_w_in[j], gmlp_v_norm[j], gmlp_w_s[j],
                               gmlp_b_s[j], mem_q_norm[i])
        x = _mix_ffn(x, tok, mq, km, vm, w_out[i], norm_ffn2[i], ffn2_w_in[i], ffn2_w_out[i])
    return x
```

```python
import math

import jax
import jax.numpy as jnp
from jax import lax
from jax.experimental import pallas as pl
from jax.experimental.pallas import tpu as pltpu

D_MODEL = 1024
D_FF = 2816
HEAD_DIM = 64
MEM_LEN = 256
MEM_HEADS = 4
MEM_WIDTH = MEM_HEADS * HEAD_DIM
TOK_WIDTH = D_MODEL - MEM_WIDTH
TOK_HEADS = TOK_WIDTH // HEAD_DIM
HEAD_PAIRS = TOK_HEADS // 2
CHUNK = 128
EPS = 1e-6

LANES = 128
MXU_WIDTH = 256
SUBLANES = 8
PAIR_WIDTH = 2 * HEAD_DIM
NEG_BIG = -1e30
QK_SCALE = 1.0 / math.sqrt(HEAD_DIM)
LOG2E = math.log2(math.e)
SPLIT = 3

FFN_TM = 1024
FFN_FC = 256
MIX_TM = 512
ATT_TK = 512
ATT_TQ = 2 * ATT_TK
VMEM_LIMIT = 56 * 1024 * 1024

KEY_BLOCK = 2 * PAIR_WIDTH
ONES_LANE0 = 2 * SPLIT
VT_BLOCK = 128

F32 = jnp.float32
BF16 = jnp.bfloat16


def _dot(a, b):
    return jnp.dot(a, b, preferred_element_type=F32)


def _dot_nt(a, b):
    return lax.dot_general(a, b, (((1,), (1,)), ((), ())), preferred_element_type=F32)


def _split3(y):
    hi = y.astype(BF16)
    r = y - hi.astype(F32)
    mid = r.astype(BF16)
    lo = (r - mid.astype(F32)).astype(BF16)
    return hi, mid, lo


def _rms_rows(x, gain):
    ms = jnp.mean(x * x, axis=-1, keepdims=True)
    return x * lax.rsqrt(ms + EPS) * gain


def _group_rms(y, p_blk, gain):
    sq = (y * y).astype(BF16)
    ss = jnp.concatenate(
        [_dot(sq[:, c:c + MXU_WIDTH], p_blk) for c in range(0, y.shape[1], MXU_WIDTH)], axis=1)
    return y * lax.rsqrt(ss * (1.0 / HEAD_DIM) + EPS) * gain


def _group_block():
    grp = jnp.arange(MXU_WIDTH) // HEAD_DIM
    return (grp[:, None] == grp[None, :]).astype(BF16)


def _const_spec(shape):
    nd = len(shape)
    return pl.BlockSpec(shape, lambda *_: (0,) * nd, pipeline_mode=pl.Buffered(1))


def _params(sem):
    return pltpu.CompilerParams(dimension_semantics=sem, vmem_limit_bytes=VMEM_LIMIT)


def _ffn_tile(x_ref, g_ref, wa_ref, wb_ref, wo_ref, h_scr, g_scr):
    h_scr[...] = _rms_rows(x_ref[...], g_ref[...]).astype(BF16)
    for c in range(D_FF // FFN_FC):
        cols = slice(c * FFN_FC, (c + 1) * FFN_FC)
        h = h_scr[...]
        a = _dot(h, wa_ref[:, cols])
        b = _dot(h, wb_ref[:, cols])
        g_scr[:, cols] = (a * jax.nn.sigmoid(a) * b).astype(BF16)
    return x_ref[...] + 0.5 * _dot(g_scr[...], wo_ref[...])


def _ffn_kernel(x_ref, g_ref, wa_ref, wb_ref, wo_ref, o_ref, h_scr, g_scr):
    o_ref[...] = _ffn_tile(x_ref, g_ref, wa_ref, wb_ref, wo_ref, h_scr, g_scr)


def _ffn_weights(w_in, w_out):
    return w_in[:, :D_FF].astype(BF16), w_in[:, D_FF:].astype(BF16), w_out.astype(BF16)


def _ffn(x2, gain, w_in, w_out):
    n = x2.shape[0]
    wa, wb, wo = _ffn_weights(w_in, w_out)
    return pl.pallas_call(
        _ffn_kernel,
        grid=(n // FFN_TM,),
        in_specs=[
            pl.BlockSpec((FFN_TM, D_MODEL), lambda i: (i, 0)),
            _const_spec((1, D_MODEL)),
            _const_spec((D_MODEL, D_FF)),
            _const_spec((D_MODEL, D_FF)),
            _const_spec((D_FF, D_MODEL)),
        ],
        out_specs=pl.BlockSpec((FFN_TM, D_MODEL), lambda i: (i, 0)),
        out_shape=jax.ShapeDtypeStruct((n, D_MODEL), F32),
        scratch_shapes=[pltpu.VMEM((FFN_TM, D_MODEL), BF16),
                        pltpu.VMEM((FFN_TM, D_FF), BF16)],
        compiler_params=_params(("arbitrary",)),
        name="ffn",
    )(x2, gain.reshape(1, D_MODEL), wa, wb, wo)


def _fox_in_kernel(x_ref, g_ref, w_ref, wvt_ref, bf_ref, gq_ref, gk_ref, gmq_ref,
                   gb_ref, tri_ref, route_ref, ones_ref,
                   q_ref, k_ref, vt_ref, mq_ref, carry_scr):
    @pl.when(pl.program_id(1) == 0)
    def _():
        carry_scr[...] = jnp.zeros_like(carry_scr)

    t = TOK_WIDTH
    h = _rms_rows(x_ref[0], g_ref[...]).astype(BF16)
    f = _dot(h, w_ref[:, 2 * t + MEM_WIDTH:]) + bf_ref[...]
    q = _dot(h, w_ref[:, 0:t])
    log_f = (jnp.minimum(f, 0.0) - jnp.log1p(jnp.exp(-jnp.abs(f)))) * LOG2E
    k = _dot(h, w_ref[:, t:2 * t])
    tri = tri_ref[...]
    c3 = _dot(tri, jnp.concatenate(_split3(log_f), axis=1))
    c = sum(c3[:, j * LANES:(j + 1) * LANES] for j in range(SPLIT)) + carry_scr[0:1, :]
    carry_scr[...] = jnp.broadcast_to(c[MIX_TM - 1:MIX_TM, :], carry_scr.shape)
    vt = _dot_nt(wvt_ref[...], h)
    mq = _dot(h, w_ref[:, 2 * t:2 * t + MEM_WIDTH])
    q_ref[0] = (_group_rms(q, gb_ref[...], gq_ref[...])
                * (QK_SCALE * LOG2E)).astype(BF16)
    decay = _dot(jnp.concatenate(_split3(c), axis=1), route_ref[...]) + ones_ref[...]
    kn = _group_rms(k, gb_ref[...], gk_ref[...])
    mq_ref[0] = (_group_rms(mq, gb_ref[...], gmq_ref[...]) * QK_SCALE).astype(BF16)

    for p in range(HEAD_PAIRS):
        pair = slice(p * PAIR_WIDTH, (p + 1) * PAIR_WIDTH)
        k_ref[0, :, p * KEY_BLOCK:p * KEY_BLOCK + PAIR_WIDTH] = kn[:, pair].astype(BF16)
        k_ref[0, :, p * KEY_BLOCK + PAIR_WIDTH:(p + 1) * KEY_BLOCK] = decay[:, pair].astype(BF16)
    ones_rows = jnp.where(
        lax.broadcasted_iota(jnp.int32, (VT_BLOCK - HEAD_DIM, MIX_TM), 0) == 0, 1.0, 0.0)
    for hd in range(TOK_HEADS):
        vt_ref[0, hd * VT_BLOCK:hd * VT_BLOCK + HEAD_DIM, :] = (
            vt[hd * HEAD_DIM:(hd + 1) * HEAD_DIM, :].astype(BF16))
        vt_ref[0, hd * VT_BLOCK + HEAD_DIM:(hd + 1) * VT_BLOCK, :] = ones_rows.astype(BF16)


def _decay_route():
    r = jnp.zeros((SPLIT * LANES, HEAD_PAIRS * PAIR_WIDTH), F32)
    for hd in range(TOK_HEADS):
        for j in range(SPLIT):
            r = r.at[j * LANES + hd, (hd // 2) * PAIR_WIDTH + (hd % 2) * SPLIT + j].set(-1.0)
    ones = jnp.zeros((1, HEAD_PAIRS * PAIR_WIDTH), F32)
    for p in range(HEAD_PAIRS):
        ones = ones.at[0, p * PAIR_WIDTH + ONES_LANE0:p * PAIR_WIDTH + ONES_LANE0 + SPLIT].set(1.0)
    return r.astype(BF16), ones


def _fox_in(x, gain, w_in, b_f, g_q, g_k, g_mq):
    b, s, _ = x.shape
    t = TOK_WIDTH
    n_gate = 3 * t
    w_all = jnp.concatenate(
        [w_in[:, :2 * t], w_in[:, n_gate + TOK_HEADS:], w_in[:, n_gate:n_gate + TOK_HEADS],
         jnp.zeros((D_MODEL, LANES - TOK_HEADS), w_in.dtype)], axis=1).astype(BF16)
    n_all = w_all.shape[1]
    wvt = w_in[:, 2 * t:n_gate].T.astype(BF16)
    bf = jnp.zeros((1, LANES), F32).at[0, :TOK_HEADS].set(b_f)
    idx = jnp.arange(MIX_TM)
    tri = (idx[None, :] <= idx[:, None]).astype(BF16)
    route, ones = _decay_route()
    tile = lambda w: pl.BlockSpec((1, MIX_TM, w), lambda bi, i: (bi, i, 0))
    return pl.pallas_call(
        _fox_in_kernel,
        grid=(b, s // MIX_TM),
        in_specs=[
            tile(D_MODEL),
            _const_spec((1, D_MODEL)),
            _const_spec((D_MODEL, n_all)),
            _const_spec((t, D_MODEL)),
            _const_spec((1, LANES)),
            _const_spec((1, t)),
            _const_spec((1, t)),
            _const_spec((1, MEM_WIDTH)),
            _const_spec((MXU_WIDTH, MXU_WIDTH)),
            _const_spec((MIX_TM, MIX_TM)),
            _const_spec(route.shape),
            _const_spec(ones.shape),
        ],
        out_specs=[tile(t), tile(HEAD_PAIRS * KEY_BLOCK),
                   pl.BlockSpec((1, TOK_HEADS * VT_BLOCK, MIX_TM), lambda bi, i: (bi, 0, i)),
                   tile(MEM_WIDTH)],
        out_shape=[jax.ShapeDtypeStruct((b, s, t), BF16),
                   jax.ShapeDtypeStruct((b, s, HEAD_PAIRS * KEY_BLOCK), BF16),
                   jax.ShapeDtypeStruct((b, TOK_HEADS * VT_BLOCK, s), BF16),
                   jax.ShapeDtypeStruct((b, s, MEM_WIDTH), BF16)],
        scratch_shapes=[pltpu.VMEM((SUBLANES, LANES), F32)],
        compiler_params=_params(("arbitrary", "arbitrary")),
        name="fox_in",
    )(x, gain.reshape(1, D_MODEL), w_all, wvt, bf,
      jnp.tile(g_q, TOK_HEADS).reshape(1, t), jnp.tile(g_k, TOK_HEADS).reshape(1, t),
      jnp.tile(g_mq, MEM_HEADS).reshape(1, MEM_WIDTH), _group_block(), tri, route, ones)


def _fox_attn_kernel(q_ref, k_ref, vt_ref, o_ref,
                     qa_scr, sa_scr, sb_scr, sc_scr, ta_scr, tb_scr, tc_scr, m_scr, acc_scr):
    n_tiles = q_ref.shape[1] // ATT_TQ
    buf_a = (sa_scr, ta_scr)
    buf_b = (sb_scr, tb_scr)
    buf_c = (sc_scr, tc_scr)

    def prep(t, slot):
        q0 = pl.multiple_of(t * ATT_TQ, ATT_TQ)
        q_t = q_ref[0, pl.ds(q0, ATT_TQ), :].astype(F32).T
        row = lax.broadcasted_iota(jnp.int32, (PAIR_WIDTH, ATT_TQ), 0)
        lane = lax.broadcasted_iota(jnp.int32, (PAIR_WIDTH, PAIR_WIDTH), 1)
        base = -k_ref[0, pl.ds(q0, 2 * SUBLANES), PAIR_WIDTH:KEY_BLOCK].astype(F32)
        base = jnp.broadcast_to(base[0:1, :], (PAIR_WIDTH, PAIR_WIDTH))
        in_ones = (lane >= ONES_LANE0) & (lane < ONES_LANE0 + SPLIT)
        for hh in range(2):
            own = (lane >= hh * SPLIT) & (lane < (hh + 1) * SPLIT)
            moved = pltpu.roll(base, ONES_LANE0 - hh * SPLIT, 1)
            qa_t = jnp.where(own, 1.0, jnp.where(in_ones, moved, 0.0)).T
            qa_scr[slot, hh, 0:PAIR_WIDTH, :] = jnp.where(
                (row < HEAD_DIM) == (hh == 0), q_t, 0.0).astype(BF16)
            qa_scr[slot, hh, PAIR_WIDTH:KEY_BLOCK, :] = jnp.concatenate(
                [qa_t] * (ATT_TQ // PAIR_WIDTH), axis=1).astype(BF16)

    def scores(slot, blk, buf, lo=0):
        s_scr, top_scr = buf
        kb = k_ref[0, pl.ds(pl.multiple_of(blk * ATT_TK, ATT_TK), ATT_TK), :]
        for hh in range(2):
            s = _dot(kb, qa_scr[slot, hh, :, lo:])
            s_scr[hh, :, lo:] = s
            top_scr[hh, :, lo:] = jnp.broadcast_to(
                jnp.max(s, axis=0, keepdims=True), (SUBLANES, ATT_TQ - lo))

    def consume(blk, buf, masked, lo=0):
        s_scr, top_scr = buf
        k0 = pl.multiple_of(blk * ATT_TK, ATT_TK)
        probs = []
        for hh in range(2):
            s = s_scr[hh, :, lo:]
            if masked:
                key = lax.broadcasted_iota(jnp.int32, s.shape, 0)
                qry = lax.broadcasted_iota(jnp.int32, s.shape, 1)
                s = jnp.where(key <= qry, s, NEG_BIG)
                top = jnp.max(s, axis=0, keepdims=True)
            else:
                top = top_scr[hh, 0:1, lo:]
            m_prev = m_scr[hh, 0:1, lo:]
            m_new = jnp.maximum(m_prev, top)
            m_scr[hh, :, lo:] = jnp.broadcast_to(m_new, (SUBLANES, ATT_TQ - lo))
            probs.append((jnp.exp2(s - m_new).astype(BF16), jnp.exp2(m_prev - m_new)))
        for hh in range(2):
            p, alpha = probs[hh]
            vt = vt_ref[0, hh * VT_BLOCK:(hh + 1) * VT_BLOCK, pl.ds(k0, ATT_TK)]
            acc_scr[hh, :, lo:] = alpha * acc_scr[hh, :, lo:] + _dot(vt, p)

    def reset():
        m_scr[...] = jnp.full_like(m_scr, NEG_BIG)
        acc_scr[...] = jnp.zeros_like(acc_scr)

    def pair(slot, blk, buf_first):
        scores(slot, blk + 1, buf_b)
        consume(blk, buf_first, False)
        scores(slot, blk + 2, buf_a)
        consume(blk + 1, buf_b, False)

    def tail(i, slot):
        scores(slot, 2 * i + 1, buf_b, ATT_TK)
        prep(jnp.minimum(i + 1, n_tiles - 1), 1 - slot)
        scores(1 - slot, 0, buf_c)
        consume(2 * i, buf_a, True)
        consume(2 * i + 1, buf_b, True, ATT_TK)
        halves = []
        for hh in range(2):
            acc = acc_scr[hh]
            halves.append(acc[0:HEAD_DIM, :] / acc[HEAD_DIM:HEAD_DIM + 1, :])
        q0 = pl.multiple_of(i * ATT_TQ, ATT_TQ)
        o_ref[0, pl.ds(q0, ATT_TQ), :] = jnp.concatenate(halves, axis=0).T.astype(o_ref.dtype)

    def tile(i, carry):
        slot = i % 2
        reset()
        pair(slot, 0, buf_c)

        def body(jj, inner):
            pair(slot, 2 * jj, buf_a)
            return inner

        lax.fori_loop(1, i, body, 0)
        tail(i, slot)
        return carry

    prep(0, 0)
    scores(0, 0, buf_a)
    reset()
    tail(0, 0)
    lax.fori_loop(1, n_tiles, tile, 0)


def _fox_attn(q, k, v):
    b, s, t = q.shape
    return pl.pallas_call(
        _fox_attn_kernel,
        grid=(b, HEAD_PAIRS),
        in_specs=[
            pl.BlockSpec((1, s, PAIR_WIDTH), lambda bi, p: (bi, 0, p)),
            pl.BlockSpec((1, s, KEY_BLOCK), lambda bi, p: (bi, 0, p)),
            pl.BlockSpec((1, 2 * VT_BLOCK, s), lambda bi, p: (bi, p, 0)),
        ],
        out_specs=pl.BlockSpec((1, s, PAIR_WIDTH), lambda bi, p: (bi, 0, p)),
        out_shape=jax.ShapeDtypeStruct((b, s, t), BF16),
        scratch_shapes=[pltpu.VMEM((2, 2, KEY_BLOCK, ATT_TQ), BF16),
                        pltpu.VMEM((2, ATT_TK, ATT_TQ), F32),
                        pltpu.VMEM((2, ATT_TK, ATT_TQ), F32),
                        pltpu.VMEM((2, ATT_TK, ATT_TQ), F32),
                        pltpu.VMEM((2, SUBLANES, ATT_TQ), F32),
                        pltpu.VMEM((2, SUBLANES, ATT_TQ), F32),
                        pltpu.VMEM((2, SUBLANES, ATT_TQ), F32),
                        pltpu.VMEM((2, SUBLANES, ATT_TQ), F32),
                        pltpu.VMEM((2, VT_BLOCK, ATT_TQ), F32)],
        compiler_params=_params(("arbitrary", "arbitrary")),
        name="fox_attn",
    )(q, k, v)


def _mem_kv_kernel(mem_ref, g_ref, w_ref, gk_ref, gb_ref, k_ref, v_ref):
    mem_n = _rms_rows(mem_ref[0], g_ref[...]).astype(BF16)
    kv = _dot(mem_n, w_ref[...])
    k_ref[0] = _group_rms(kv[:, :MEM_WIDTH], gb_ref[...], gk_ref[...]).T.astype(BF16)
    v_ref[0] = kv[:, MEM_WIDTH:].astype(BF16)


def _mem_kv(mem, g_mem, w_kv, g_k):
    b = mem.shape[0]
    blk = pl.BlockSpec((1, MEM_LEN, MEM_WIDTH), lambda bi: (bi, 0, 0))
    return pl.pallas_call(
        _mem_kv_kernel,
        grid=(b,),
        in_specs=[
            pl.BlockSpec((1, MEM_LEN, D_MODEL), lambda bi: (bi, 0, 0)),
            _const_spec((1, D_MODEL)),
            _const_spec((D_MODEL, 2 * MEM_WIDTH)),
            _const_spec((1, MEM_WIDTH)),
            _const_spec((MXU_WIDTH, MXU_WIDTH)),
        ],
        out_specs=[blk, blk],
        out_shape=[jax.ShapeDtypeStruct((b, MEM_LEN, MEM_WIDTH), BF16)] * 2,
        compiler_params=_params(("arbitrary",)),
        name="mem_kv",
    )(mem, g_mem.reshape(1, D_MODEL), w_kv.astype(BF16),
      jnp.tile(g_k, MEM_HEADS).reshape(1, MEM_WIDTH), _group_block())


def _mix_ffn_kernel(x_ref, tok_ref, mq_ref, km_ref, vm_ref, wt_ref, wm_ref,
                    g_ref, wa_ref, wb_ref, wo_ref, o_ref, x1_scr, h_scr, g_scr):
    mq = mq_ref[0]
    km = km_ref[0]
    vm = vm_ref[0]
    lane = lax.broadcasted_iota(jnp.int32, (MIX_TM, MEM_WIDTH), 1) // HEAD_DIM
    zero = jnp.zeros_like(mq)
    scores = [_dot(jnp.where(lane == hh, mq, zero), km) for hh in range(MEM_HEADS)]
    y = _dot(tok_ref[0], wt_ref[...])
    probs = []
    for s in scores:
        p = jnp.exp(s - jnp.max(s, axis=-1, keepdims=True))
        probs.append((p / jnp.sum(p, axis=-1, keepdims=True)).astype(BF16))
    mo = jnp.zeros((MIX_TM, MEM_WIDTH), F32)
    for hh in range(MEM_HEADS):
        mo = jnp.where(lane == hh, _dot(probs[hh], vm), mo)
    x1_scr[...] = x_ref[0] + y + _dot(mo.astype(BF16), wm_ref[...])
    o_ref[0] = _ffn_tile(x1_scr, g_ref, wa_ref, wb_ref, wo_ref, h_scr, g_scr)


def _mix_ffn(x, tok, mq, km, vm, w_out, gain, w_in, w_ffn_out):
    b, s, _ = x.shape
    wt = w_out[:TOK_WIDTH].astype(BF16)
    wm = w_out[TOK_WIDTH:].astype(BF16)
    wa, wb, wo = _ffn_weights(w_in, w_ffn_out)
    tile = lambda w: pl.BlockSpec((1, MIX_TM, w), lambda bi, i: (bi, i, 0))
    memblk = pl.BlockSpec((1, MEM_LEN, MEM_WIDTH), lambda bi, i: (bi, 0, 0))
    return pl.pallas_call(
        _mix_ffn_kernel,
        grid=(b, s // MIX_TM),
        in_specs=[tile(D_MODEL), tile(TOK_WIDTH), tile(MEM_WIDTH), memblk, memblk,
                  _const_spec((TOK_WIDTH, D_MODEL)), _const_spec((MEM_WIDTH, D_MODEL)),
                  _const_spec((1, D_MODEL)), _const_spec((D_MODEL, D_FF)),
                  _const_spec((D_MODEL, D_FF)), _const_spec((D_FF, D_MODEL))],
        out_specs=tile(D_MODEL),
        out_shape=jax.ShapeDtypeStruct((b, s, D_MODEL), F32),
        scratch_shapes=[pltpu.VMEM((MIX_TM, D_MODEL), F32),
                        pltpu.VMEM((MIX_TM, D_MODEL), BF16),
                        pltpu.VMEM((MIX_TM, D_FF), BF16)],
        compiler_params=_params(("arbitrary", "arbitrary")),
        name="mix_ffn",
    )(x, tok, mq, km, vm, wt, wm, gain.reshape(1, D_MODEL), wa, wb, wo)


def _gelu_tanh(x):
    return 0.5 * x * (1.0 + jnp.tanh(math.sqrt(2.0 / math.pi) * (x + 0.044715 * (x * x * x))))


def _gmlp_in_kernel(x_ref, g_ref, wuv_ref, wmq_ref, gv_ref, gmq_ref, ws_ref, bs_ref,
                    gb_ref, tok_ref, mq_ref, vn_scr, u_scr):
    h = _rms_rows(x_ref[0], g_ref[...]).astype(BF16)
    t = TOK_WIDTH
    n_chunk = MIX_TM // CHUNK
    v = _dot(h, wuv_ref[:, t:2 * t])
    mq = _dot(h, wmq_ref[...])
    u_scr[...] = _gelu_tanh(_dot(h, wuv_ref[:, 0:t]))
    vn_scr[...] = _group_rms(_gelu_tanh(v), gb_ref[...], gv_ref[...]).astype(BF16)
    mq_ref[0] = (_group_rms(mq, gb_ref[...], gmq_ref[...]) * QK_SCALE).astype(BF16)

    row = lax.broadcasted_iota(jnp.int32, (CHUNK, CHUNK), 0)
    col = lax.broadcasted_iota(jnp.int32, (CHUNK, CHUNK), 1)
    causal = col <= row
    first = lax.broadcasted_iota(jnp.int32, (CHUNK, PAIR_WIDTH), 1) < HEAD_DIM
    for p in range(HEAD_PAIRS):
        lanes = slice(p * PAIR_WIDTH, (p + 1) * PAIR_WIDTH)
        v_cat = jnp.concatenate(
            [vn_scr[c * CHUNK:(c + 1) * CHUNK, lanes] for c in range(n_chunk)], axis=1)
        wa = jnp.where(causal, ws_ref[2 * p], 0.0).astype(BF16)
        wb = jnp.where(causal, ws_ref[2 * p + 1], 0.0).astype(BF16)
        ga = _dot(wa, v_cat)
        gb = _dot(wb, v_cat)
        bias = bs_ref[:, lanes]
        for c in range(n_chunk):
            cols = slice(c * PAIR_WIDTH, (c + 1) * PAIR_WIDTH)
            gate = jnp.where(first, ga[:, cols], gb[:, cols]) + bias
            rows = slice(c * CHUNK, (c + 1) * CHUNK)
            tok_ref[0, rows, lanes] = (u_scr[rows, lanes] * gate).astype(BF16)


def _gmlp_in(x, gain, w_in, v_gain, w_s, b_s, g_mq):
    b, s, _ = x.shape
    t = TOK_WIDTH
    wuv = w_in[:, :2 * t].astype(BF16)
    wmq = w_in[:, 2 * t:].astype(BF16)
    bs_full = jnp.repeat(b_s.T, HEAD_DIM, axis=1)
    tile = lambda w: pl.BlockSpec((1, MIX_TM, w), lambda bi, i: (bi, i, 0))
    return pl.pallas_call(
        _gmlp_in_kernel,
        grid=(b, s // MIX_TM),
        in_specs=[
            tile(D_MODEL),
            _const_spec((1, D_MODEL)),
            _const_spec((D_MODEL, 2 * t)),
            _const_spec((D_MODEL, MEM_WIDTH)),
            _const_spec((1, t)),
            _const_spec((1, MEM_WIDTH)),
            _const_spec((TOK_HEADS, CHUNK, CHUNK)),
            _const_spec((CHUNK, t)),
            _const_spec((MXU_WIDTH, MXU_WIDTH)),
        ],
        out_specs=[tile(t), tile(MEM_WIDTH)],
        out_shape=[jax.ShapeDtypeStruct((b, s, t), BF16),
                   jax.ShapeDtypeStruct((b, s, MEM_WIDTH), BF16)],
        scratch_shapes=[pltpu.VMEM((MIX_TM, t), BF16), pltpu.VMEM((MIX_TM, t), F32)],
        compiler_params=_params(("arbitrary", "arbitrary")),
        name="gmlp_in",
    )(x, gain.reshape(1, D_MODEL), wuv, wmq, v_gain.reshape(1, t),
      jnp.tile(g_mq, MEM_HEADS).reshape(1, MEM_WIDTH), w_s, bs_full, _group_block())


def kernel(x, mem, norm_ffn1, ffn1_w_in, ffn1_w_out, norm_mix, norm_ffn2, ffn2_w_in, ffn2_w_out,
           w_out, mem_norm, mem_w_kv, mem_q_norm, mem_k_norm, fox_w_in, fox_b_f, fox_q_norm,
           fox_k_norm, gmlp_w_in, gmlp_v_norm, gmlp_w_s, gmlp_b_s):
    b, s, d = x.shape
    depth = norm_mix.shape[0]
    for i in range(depth):
        kind, j = i % 2, i // 2
        x = _ffn(x.reshape(b * s, d), norm_ffn1[i], ffn1_w_in[i], ffn1_w_out[i]).reshape(b, s, d)
        km, vm = _mem_kv(mem, mem_norm, mem_w_kv[i], mem_k_norm[i])
        if kind == 0:
            q, k, v, mq = _fox_in(x, norm_mix[i], fox_w_in[j], fox_b_f[j], fox_q_norm[j],
                                  fox_k_norm[j], mem_q_norm[i])
            tok = _fox_attn(q, k, v)
        else:
            tok, mq = _gmlp_in(x, norm_mix[i], gmlp_w_in[j], gmlp_v_norm[j], gmlp_w_s[j],
                               gmlp_b_s[j], mem_q_norm[i])
        x = _mix_ffn(x, tok, mq, km, vm, w_out[i], norm_ffn2[i], ffn2_w_in[i], ffn2_w_out[i])
    return x
```

```python
import math

import jax
import jax.numpy as jnp
from jax import lax
from jax.experimental import pallas as pl
from jax.experimental.pallas import tpu as pltpu

D_MODEL = 1024
D_FF = 2816
HEAD_DIM = 64
MEM_LEN = 256
MEM_HEADS = 4
MEM_WIDTH = MEM_HEADS * HEAD_DIM
TOK_WIDTH = D_MODEL - MEM_WIDTH
TOK_HEADS = TOK_WIDTH // HEAD_DIM
HEAD_PAIRS = TOK_HEADS // 2
CHUNK = 128
EPS = 1e-6

LANES = 128
MXU_WIDTH = 256
SUBLANES = 8
PAIR_WIDTH = 2 * HEAD_DIM
NEG_BIG = -1e30
QK_SCALE = 1.0 / math.sqrt(HEAD_DIM)
LOG2E = math.log2(math.e)
SPLIT = 3

FFN_TM = 1024
FFN_FC = 256
MIX_TM = 512
ATT_TK = 512
ATT_TQ = 2 * ATT_TK
VMEM_LIMIT = 56 * 1024 * 1024

KEY_BLOCK = 2 * PAIR_WIDTH
ONES_LANE0 = 2 * SPLIT
VT_BLOCK = 128

F32 = jnp.float32
BF16 = jnp.bfloat16


def _dot(a, b):
    return jnp.dot(a, b, preferred_element_type=F32)


def _dot_nt(a, b):
    return lax.dot_general(a, b, (((1,), (1,)), ((), ())), preferred_element_type=F32)


def _split3(y):
    hi = y.astype(BF16)
    r = y - hi.astype(F32)
    mid = r.astype(BF16)
    lo = (r - mid.astype(F32)).astype(BF16)
    return hi, mid, lo


def _rms_rows(x, gain):
    ms = jnp.mean(x * x, axis=-1, keepdims=True)
    return x * lax.rsqrt(ms + EPS) * gain


def _group_rms(y, p_blk, gain):
    sq = (y * y).astype(BF16)
    ss = jnp.concatenate(
        [_dot(sq[:, c:c + MXU_WIDTH], p_blk) for c in range(0, y.shape[1], MXU_WIDTH)], axis=1)
    return y * lax.rsqrt(ss * (1.0 / HEAD_DIM) + EPS) * gain


def _group_block():
    grp = jnp.arange(MXU_WIDTH) // HEAD_DIM
    return (grp[:, None] == grp[None, :]).astype(BF16)


def _const_spec(shape):
    nd = len(shape)
    return pl.BlockSpec(shape, lambda *_: (0,) * nd, pipeline_mode=pl.Buffered(1))


def _params(sem):
    return pltpu.CompilerParams(dimension_semantics=sem, vmem_limit_bytes=VMEM_LIMIT)


def _ffn_tile(x_ref, g_ref, wa_ref, wb_ref, wo_ref, h_scr, g_scr):
    h_scr[...] = _rms_rows(x_ref[...], g_ref[...]).astype(BF16)
    for c in range(D_FF // FFN_FC):
        cols = slice(c * FFN_FC, (c + 1) * FFN_FC)
        h = h_scr[...]
        a = _dot(h, wa_ref[:, cols])
        b = _dot(h, wb_ref[:, cols])
        g_scr[:, cols] = (a * jax.nn.sigmoid(a) * b).astype(BF16)
    return x_ref[...] + 0.5 * _dot(g_scr[...], wo_ref[...])


def _ffn_kernel(x_ref, g_ref, wa_ref, wb_ref, wo_ref, o_ref, h_scr, g_scr):
    o_ref[...] = _ffn_tile(x_ref, g_ref, wa_ref, wb_ref, wo_ref, h_scr, g_scr)


def _ffn_weights(w_in, w_out):
    return w_in[:, :D_FF].astype(BF16), w_in[:, D_FF:].astype(BF16), w_out.astype(BF16)


def _ffn(x2, gain, w_in, w_out):
    n = x2.shape[0]
    wa, wb, wo = _ffn_weights(w_in, w_out)
    return pl.pallas_call(
        _ffn_kernel,
        grid=(n // FFN_TM,),
        in_specs=[
            pl.BlockSpec((FFN_TM, D_MODEL), lambda i: (i, 0)),
            _const_spec((1, D_MODEL)),
            _const_spec((D_MODEL, D_FF)),
            _const_spec((D_MODEL, D_FF)),
            _const_spec((D_FF, D_MODEL)),
        ],
        out_specs=pl.BlockSpec((FFN_TM, D_MODEL), lambda i: (i, 0)),
        out_shape=jax.ShapeDtypeStruct((n, D_MODEL), F32),
        scratch_shapes=[pltpu.VMEM((FFN_TM, D_MODEL), BF16),
                        pltpu.VMEM((FFN_TM, D_FF), BF16)],
        compiler_params=_params(("arbitrary",)),
        name="ffn",
    )(x2, gain.reshape(1, D_MODEL), wa, wb, wo)


def _fox_in_kernel(x_ref, g_ref, wqk_ref, wmg_ref, wvt_ref, bf_ref, gq_ref, gk_ref, gmq_ref,
                   gb_ref, tri_ref, route_ref, ones_ref,
                   q_ref, k_ref, vt_ref, mq_ref, carry_scr):
    @pl.when(pl.program_id(1) == 0)
    def _():
        carry_scr[...] = jnp.zeros_like(carry_scr)

    t = TOK_WIDTH
    h = _rms_rows(x_ref[0], g_ref[...]).astype(BF16)
    f = _dot(h, wmg_ref[:, MEM_WIDTH:]) + bf_ref[...]
    q = _dot(h, wqk_ref[:, 0:t])
    log_f = (jnp.minimum(f, 0.0) - jnp.log1p(jnp.exp(-jnp.abs(f)))) * LOG2E
    k = _dot(h, wqk_ref[:, t:2 * t])
    tri = tri_ref[...]
    c3 = _dot(tri, jnp.concatenate(_split3(log_f), axis=1))
    c = sum(c3[:, j * LANES:(j + 1) * LANES] for j in range(SPLIT)) + carry_scr[0:1, :]
    carry_scr[...] = jnp.broadcast_to(c[MIX_TM - 1:MIX_TM, :], carry_scr.shape)
    vt = _dot_nt(wvt_ref[...], h)
    mq = _dot(h, wmg_ref[:, 0:MEM_WIDTH])
    q_ref[0] = (_group_rms(q, gb_ref[...], gq_ref[...])
                * (QK_SCALE * LOG2E)).astype(BF16)
    decay = _dot(jnp.concatenate(_split3(c), axis=1), route_ref[...]) + ones_ref[...]
    kn = _group_rms(k, gb_ref[...], gk_ref[...])
    mq_ref[0] = (_group_rms(mq, gb_ref[...], gmq_ref[...]) * QK_SCALE).astype(BF16)

    for p in range(HEAD_PAIRS):
        pair = slice(p * PAIR_WIDTH, (p + 1) * PAIR_WIDTH)
        k_ref[0, :, p * KEY_BLOCK:p * KEY_BLOCK + PAIR_WIDTH] = kn[:, pair].astype(BF16)
        k_ref[0, :, p * KEY_BLOCK + PAIR_WIDTH:(p + 1) * KEY_BLOCK] = decay[:, pair].astype(BF16)
    ones_rows = jnp.where(
        lax.broadcasted_iota(jnp.int32, (VT_BLOCK - HEAD_DIM, MIX_TM), 0) == 0, 1.0, 0.0)
    for hd in range(TOK_HEADS):
        vt_ref[0, hd * VT_BLOCK:hd * VT_BLOCK + HEAD_DIM, :] = (
            vt[hd * HEAD_DIM:(hd + 1) * HEAD_DIM, :].astype(BF16))
        vt_ref[0, hd * VT_BLOCK + HEAD_DIM:(hd + 1) * VT_BLOCK, :] = ones_rows.astype(BF16)


def _decay_route():
    r = jnp.zeros((SPLIT * LANES, HEAD_PAIRS * PAIR_WIDTH), F32)
    for hd in range(TOK_HEADS):
        for j in range(SPLIT):
            r = r.at[j * LANES + hd, (hd // 2) * PAIR_WIDTH + (hd % 2) * SPLIT + j].set(-1.0)
    ones = jnp.zeros((1, HEAD_PAIRS * PAIR_WIDTH), F32)
    for p in range(HEAD_PAIRS):
        ones = ones.at[0, p * PAIR_WIDTH + ONES_LANE0:p * PAIR_WIDTH + ONES_LANE0 + SPLIT].set(1.0)
    return r.astype(BF16), ones


def _fox_in(x, gain, w_in, b_f, g_q, g_k, g_mq):
    b, s, _ = x.shape
    t = TOK_WIDTH
    n_gate = 3 * t
    wqk = w_in[:, :2 * t].astype(BF16)
    wmg = jnp.concatenate(
        [w_in[:, n_gate + TOK_HEADS:], w_in[:, n_gate:n_gate + TOK_HEADS],
         jnp.zeros((D_MODEL, LANES - TOK_HEADS), w_in.dtype)], axis=1).astype(BF16)
    wvt = w_in[:, 2 * t:n_gate].T.astype(BF16)
    bf = jnp.zeros((1, LANES), F32).at[0, :TOK_HEADS].set(b_f)
    idx = jnp.arange(MIX_TM)
    tri = (idx[None, :] <= idx[:, None]).astype(BF16)
    route, ones = _decay_route()
    tile = lambda w: pl.BlockSpec((1, MIX_TM, w), lambda bi, i: (bi, i, 0))
    return pl.pallas_call(
        _fox_in_kernel,
        grid=(b, s // MIX_TM),
        in_specs=[
            tile(D_MODEL),
            _const_spec((1, D_MODEL)),
            _const_spec((D_MODEL, 2 * t)),
            _const_spec((D_MODEL, MEM_WIDTH + LANES)),
            _const_spec((t, D_MODEL)),
            _const_spec((1, LANES)),
            _const_spec((1, t)),
            _const_spec((1, t)),
            _const_spec((1, MEM_WIDTH)),
            _const_spec((MXU_WIDTH, MXU_WIDTH)),
            _const_spec((MIX_TM, MIX_TM)),
            _const_spec(route.shape),
            _const_spec(ones.shape),
        ],
        out_specs=[tile(t), tile(HEAD_PAIRS * KEY_BLOCK),
                   pl.BlockSpec((1, TOK_HEADS * VT_BLOCK, MIX_TM), lambda bi, i: (bi, 0, i)),
                   tile(MEM_WIDTH)],
        out_shape=[jax.ShapeDtypeStruct((b, s, t), BF16),
                   jax.ShapeDtypeStruct((b, s, HEAD_PAIRS * KEY_BLOCK), BF16),
                   jax.ShapeDtypeStruct((b, TOK_HEADS * VT_BLOCK, s), BF16),
                   jax.ShapeDtypeStruct((b, s, MEM_WIDTH), BF16)],
        scratch_shapes=[pltpu.VMEM((SUBLANES, LANES), F32)],
        compiler_params=_params(("arbitrary", "arbitrary")),
        name="fox_in",
    )(x, gain.reshape(1, D_MODEL), wqk, wmg, wvt, bf,
      jnp.tile(g_q, TOK_HEADS).reshape(1, t), jnp.tile(g_k, TOK_HEADS).reshape(1, t),
      jnp.tile(g_mq, MEM_HEADS).reshape(1, MEM_WIDTH), _group_block(), tri, route, ones)


def _fox_attn_kernel(q_ref, k_ref, vt_ref, o_ref,
                     qa_scr, sa_scr, sb_scr, sc_scr, ta_scr, tb_scr, tc_scr, m_scr, acc_scr):
    n_tiles = q_ref.shape[1] // ATT_TQ
    buf_a = (sa_scr, ta_scr)
    buf_b = (sb_scr, tb_scr)
    buf_c = (sc_scr, tc_scr)

    def prep(t, slot):
        q0 = pl.multiple_of(t * ATT_TQ, ATT_TQ)
        q_t = q_ref[0, pl.ds(q0, ATT_TQ), :].astype(F32).T
        row = lax.broadcasted_iota(jnp.int32, (PAIR_WIDTH, ATT_TQ), 0)
        lane = lax.broadcasted_iota(jnp.int32, (PAIR_WIDTH, PAIR_WIDTH), 1)
        base = -k_ref[0, pl.ds(q0, 2 * SUBLANES), PAIR_WIDTH:KEY_BLOCK].astype(F32)
        base = jnp.broadcast_to(base[0:1, :], (PAIR_WIDTH, PAIR_WIDTH))
        in_ones = (lane >= ONES_LANE0) & (lane < ONES_LANE0 + SPLIT)
        for hh in range(2):
            own = (lane >= hh * SPLIT) & (lane < (hh + 1) * SPLIT)
            moved = pltpu.roll(base, ONES_LANE0 - hh * SPLIT, 1)
            qa_t = jnp.where(own, 1.0, jnp.where(in_ones, moved, 0.0)).T
            qa_scr[slot, hh, 0:PAIR_WIDTH, :] = jnp.where(
                (row < HEAD_DIM) == (hh == 0), q_t, 0.0).astype(BF16)
            qa_scr[slot, hh, PAIR_WIDTH:KEY_BLOCK, :] = jnp.concatenate(
                [qa_t] * (ATT_TQ // PAIR_WIDTH), axis=1).astype(BF16)

    def scores(slot, blk, buf, lo=0):
        s_scr, top_scr = buf
        kb = k_ref[0, pl.ds(pl.multiple_of(blk * ATT_TK, ATT_TK), ATT_TK), :]
        for hh in range(2):
            s = _dot(kb, qa_scr[slot, hh, :, lo:])
            s_scr[hh, :, lo:] = s
            top_scr[hh, :, lo:] = jnp.broadcast_to(
                jnp.max(s, axis=0, keepdims=True), (SUBLANES, ATT_TQ - lo))

    def consume(blk, buf, masked, lo=0):
        s_scr, top_scr = buf
        k0 = pl.multiple_of(blk * ATT_TK, ATT_TK)
        probs = []
        for hh in range(2):
            s = s_scr[hh, :, lo:]
            if masked:
                key = lax.broadcasted_iota(jnp.int32, s.shape, 0)
                qry = lax.broadcasted_iota(jnp.int32, s.shape, 1)
                s = jnp.where(key <= qry, s, NEG_BIG)
                top = jnp.max(s, axis=0, keepdims=True)
            else:
                top = top_scr[hh, 0:1, lo:]
            m_prev = m_scr[hh, 0:1, lo:]
            m_new = jnp.maximum(m_prev, top)
            m_scr[hh, :, lo:] = jnp.broadcast_to(m_new, (SUBLANES, ATT_TQ - lo))
            probs.append((jnp.exp2(s - m_new).astype(BF16), jnp.exp2(m_prev - m_new)))
        for hh in range(2):
            p, alpha = probs[hh]
            vt = vt_ref[0, hh * VT_BLOCK:(hh + 1) * VT_BLOCK, pl.ds(k0, ATT_TK)]
            acc_scr[hh, :, lo:] = alpha * acc_scr[hh, :, lo:] + _dot(vt, p)

    def reset():
        m_scr[...] = jnp.full_like(m_scr, NEG_BIG)
        acc_scr[...] = jnp.zeros_like(acc_scr)

    def pair(slot, blk, buf_first):
        scores(slot, blk + 1, buf_b)
        consume(blk, buf_first, False)
        scores(slot, blk + 2, buf_a)
        consume(blk + 1, buf_b, False)

    def tail(i, slot):
        scores(slot, 2 * i + 1, buf_b, ATT_TK)
        prep(jnp.minimum(i + 1, n_tiles - 1), 1 - slot)
        scores(1 - slot, 0, buf_c)
        consume(2 * i, buf_a, True)
        consume(2 * i + 1, buf_b, True, ATT_TK)
        halves = []
        for hh in range(2):
            acc = acc_scr[hh]
            halves.append(acc[0:HEAD_DIM, :] / acc[HEAD_DIM:HEAD_DIM + 1, :])
        q0 = pl.multiple_of(i * ATT_TQ, ATT_TQ)
        o_ref[0, pl.ds(q0, ATT_TQ), :] = jnp.concatenate(halves, axis=0).T.astype(o_ref.dtype)

    def tile(i, carry):
        slot = i % 2
        reset()
        pair(slot, 0, buf_c)

        def body(jj, inner):
            pair(slot, 2 * jj, buf_a)
            return inner

        lax.fori_loop(1, i, body, 0)
        tail(i, slot)
        return carry

    prep(0, 0)
    scores(0, 0, buf_a)
    reset()
    tail(0, 0)
    lax.fori_loop(1, n_tiles, tile, 0)


def _fox_attn(q, k, v):
    b, s, t = q.shape
    return pl.pallas_call(
        _fox_attn_kernel,
        grid=(b, HEAD_PAIRS),
        in_specs=[
            pl.BlockSpec((1, s, PAIR_WIDTH), lambda bi, p: (bi, 0, p)),
            pl.BlockSpec((1, s, KEY_BLOCK), lambda bi, p: (bi, 0, p)),
            pl.BlockSpec((1, 2 * VT_BLOCK, s), lambda bi, p: (bi, p, 0)),
        ],
        out_specs=pl.BlockSpec((1, s, PAIR_WIDTH), lambda bi, p: (bi, 0, p)),
        out_shape=jax.ShapeDtypeStruct((b, s, t), BF16),
        scratch_shapes=[pltpu.VMEM((2, 2, KEY_BLOCK, ATT_TQ), BF16),
                        pltpu.VMEM((2, ATT_TK, ATT_TQ), F32),
                        pltpu.VMEM((2, ATT_TK, ATT_TQ), F32),
                        pltpu.VMEM((2, ATT_TK, ATT_TQ), F32),
                        pltpu.VMEM((2, SUBLANES, ATT_TQ), F32),
                        pltpu.VMEM((2, SUBLANES, ATT_TQ), F32),
                        pltpu.VMEM((2, SUBLANES, ATT_TQ), F32),
                        pltpu.VMEM((2, SUBLANES, ATT_TQ), F32),
                        pltpu.VMEM((2, VT_BLOCK, ATT_TQ), F32)],
        compiler_params=_params(("arbitrary", "arbitrary")),
        name="fox_attn",
    )(q, k, v)


def _mem_kv_kernel(mem_ref, g_ref, w_ref, gk_ref, gb_ref, k_ref, v_ref):
    mem_n = _rms_rows(mem_ref[0], g_ref[...]).astype(BF16)
    kv = _dot(mem_n, w_ref[...])
    k_ref[0] = _group_rms(kv[:, :MEM_WIDTH], gb_ref[...], gk_ref[...]).T.astype(BF16)
    v_ref[0] = kv[:, MEM_WIDTH:].astype(BF16)


def _mem_kv(mem, g_mem, w_kv, g_k):
    b = mem.shape[0]
    blk = pl.BlockSpec((1, MEM_LEN, MEM_WIDTH), lambda bi: (bi, 0, 0))
    return pl.pallas_call(
        _mem_kv_kernel,
        grid=(b,),
        in_specs=[
            pl.BlockSpec((1, MEM_LEN, D_MODEL), lambda bi: (bi, 0, 0)),
            _const_spec((1, D_MODEL)),
            _const_spec((D_MODEL, 2 * MEM_WIDTH)),
            _const_spec((1, MEM_WIDTH)),
            _const_spec((MXU_WIDTH, MXU_WIDTH)),
        ],
        out_specs=[blk, blk],
        out_shape=[jax.ShapeDtypeStruct((b, MEM_LEN, MEM_WIDTH), BF16)] * 2,
        compiler_params=_params(("arbitrary",)),
        name="mem_kv",
    )(mem, g_mem.reshape(1, D_MODEL), w_kv.astype(BF16),
      jnp.tile(g_k, MEM_HEADS).reshape(1, MEM_WIDTH), _group_block())


def _mix_ffn_kernel(x_ref, tok_ref, mq_ref, km_ref, vm_ref, wt_ref, wm_ref,
                    g_ref, wa_ref, wb_ref, wo_ref, o_ref, x1_scr, h_scr, g_scr):
    mq = mq_ref[0]
    km = km_ref[0]
    vm = vm_ref[0]
    lane = lax.broadcasted_iota(jnp.int32, (MIX_TM, MEM_WIDTH), 1) // HEAD_DIM
    zero = jnp.zeros_like(mq)
    scores = [_dot(jnp.where(lane == hh, mq, zero), km) for hh in range(MEM_HEADS)]
    y = _dot(tok_ref[0], wt_ref[...])
    probs = []
    for s in scores:
        p = jnp.exp(s - jnp.max(s, axis=-1, keepdims=True))
        probs.append((p / jnp.sum(p, axis=-1, keepdims=True)).astype(BF16))
    mo = jnp.zeros((MIX_TM, MEM_WIDTH), F32)
    for hh in range(MEM_HEADS):
        mo = jnp.where(lane == hh, _dot(probs[hh], vm), mo)
    x1_scr[...] = x_ref[0] + y + _dot(mo.astype(BF16), wm_ref[...])
    o_ref[0] = _ffn_tile(x1_scr, g_ref, wa_ref, wb_ref, wo_ref, h_scr, g_scr)


def _mix_ffn(x, tok, mq, km, vm, w_out, gain, w_in, w_ffn_out):
    b, s, _ = x.shape
    wt = w_out[:TOK_WIDTH].astype(BF16)
    wm = w_out[TOK_WIDTH:].astype(BF16)
    wa, wb, wo = _ffn_weights(w_in, w_ffn_out)
    tile = lambda w: pl.BlockSpec((1, MIX_TM, w), lambda bi, i: (bi, i, 0))
    memblk = pl.BlockSpec((1, MEM_LEN, MEM_WIDTH), lambda bi, i: (bi, 0, 0))
    return pl.pallas_call(
        _mix_ffn_kernel,
        grid=(b, s // MIX_TM),
        in_specs=[tile(D_MODEL), tile(TOK_WIDTH), tile(MEM_WIDTH), memblk, memblk,
                  _const_spec((TOK_WIDTH, D_MODEL)), _const_spec((MEM_WIDTH, D_MODEL)),
                  _const_spec((1, D_MODEL)), _const_spec((D_MODEL, D_FF)),
                  _const_spec((D_MODEL, D_FF)), _const_spec((D_FF, D_MODEL))],
        out_specs=tile(D_MODEL),
        out_shape=jax.ShapeDtypeStruct((b, s, D_MODEL), F32),
        scratch_shapes=[pltpu.VMEM((MIX_TM, D_MODEL), F32),
                        pltpu.VMEM((MIX_TM, D_MODEL), BF16),
                        pltpu.VMEM((MIX_TM, D_FF), BF16)],
        compiler_params=_params(("arbitrary", "arbitrary")),
        name="mix_ffn",
    )(x, tok, mq, km, vm, wt, wm, gain.reshape(1, D_MODEL), wa, wb, wo)


def _gelu_tanh(x):
    return 0.5 * x * (1.0 + jnp.tanh(math.sqrt(2.0 / math.pi) * (x + 0.044715 * (x * x * x))))


def _gmlp_in_kernel(x_ref, g_ref, wuv_ref, wmq_ref, gv_ref, gmq_ref, ws_ref, bs_ref,
                    gb_ref, tok_ref, mq_ref, vn_scr, u_scr):
    h = _rms_rows(x_ref[0], g_ref[...]).astype(BF16)
    t = TOK_WIDTH
    n_chunk = MIX_TM // CHUNK
    v = _dot(h, wuv_ref[:, t:2 * t])
    mq = _dot(h, wmq_ref[...])
    u_scr[...] = _gelu_tanh(_dot(h, wuv_ref[:, 0:t]))
    vn_scr[...] = _group_rms(_gelu_tanh(v), gb_ref[...], gv_ref[...]).astype(BF16)
    mq_ref[0] = (_group_rms(mq, gb_ref[...], gmq_ref[...]) * QK_SCALE).astype(BF16)

    row = lax.broadcasted_iota(jnp.int32, (CHUNK, CHUNK), 0)
    col = lax.broadcasted_iota(jnp.int32, (CHUNK, CHUNK), 1)
    causal = col <= row
    first = lax.broadcasted_iota(jnp.int32, (CHUNK, PAIR_WIDTH), 1) < HEAD_DIM
    for p in range(HEAD_PAIRS):
        lanes = slice(p * PAIR_WIDTH, (p + 1) * PAIR_WIDTH)
        v_cat = jnp.concatenate(
            [vn_scr[c * CHUNK:(c + 1) * CHUNK, lanes] for c in range(n_chunk)], axis=1)
        wa = jnp.where(causal, ws_ref[2 * p], 0.0).astype(BF16)
        wb = jnp.where(causal, ws_ref[2 * p + 1], 0.0).astype(BF16)
        ga = _dot(wa, v_cat)
        gb = _dot(wb, v_cat)
        bias = bs_ref[:, lanes]
        for c in range(n_chunk):
            cols = slice(c * PAIR_WIDTH, (c + 1) * PAIR_WIDTH)
            gate = jnp.where(first, ga[:, cols], gb[:, cols]) + bias
            rows = slice(c * CHUNK, (c + 1) * CHUNK)
            tok_ref[0, rows, lanes] = (u_scr[rows, lanes] * gate).astype(BF16)


def _gmlp_in(x, gain, w_in, v_gain, w_s, b_s, g_mq):
    b, s, _ = x.shape
    t = TOK_WIDTH
    wuv = w_in[:, :2 * t].astype(BF16)
    wmq = w_in[:, 2 * t:].astype(BF16)
    bs_full = jnp.repeat(b_s.T, HEAD_DIM, axis=1)
    tile = lambda w: pl.BlockSpec((1, MIX_TM, w), lambda bi, i: (bi, i, 0))
    return pl.pallas_call(
        _gmlp_in_kernel,
        grid=(b, s // MIX_TM),
        in_specs=[
            tile(D_MODEL),
            _const_spec((1, D_MODEL)),
            _const_spec((D_MODEL, 2 * t)),
            _const_spec((D_MODEL, MEM_WIDTH)),
            _const_spec((1, t)),
            _const_spec((1, MEM_WIDTH)),
            _const_spec((TOK_HEADS, CHUNK, CHUNK)),
            _const_spec((CHUNK, t)),
            _const_spec((MXU_WIDTH, MXU_WIDTH)),
        ],
        out_specs=[tile(t), tile(MEM_WIDTH)],
        out_shape=[jax.ShapeDtypeStruct((b, s, t), BF16),
                   jax.ShapeDtypeStruct((b, s, MEM_WIDTH), BF16)],
        scratch_shapes=[pltpu.VMEM((MIX_TM, t), BF16), pltpu.VMEM((MIX_TM, t), F32)],
        compiler_params=_params(("arbitrary", "arbitrary")),
        name="gmlp_in",
    )(x, gain.reshape(1, D_MODEL), wuv, wmq, v_gain.reshape(1, t),
      jnp.tile(g_mq, MEM_HEADS).reshape(1, MEM_WIDTH), w_s, bs_full, _group_block())


def kernel(x, mem, norm_ffn1, ffn1_w_in, ffn1_w_out, norm_mix, norm_ffn2, ffn2_w_in, ffn2_w_out,
           w_out, mem_norm, mem_w_kv, mem_q_norm, mem_k_norm, fox_w_in, fox_b_f, fox_q_norm,
           fox_k_norm, gmlp_w_in, gmlp_v_norm, gmlp_w_s, gmlp_b_s):
    b, s, d = x.shape
    depth = norm_mix.shape[0]
    assert d == D_MODEL and mem.shape == (b, MEM_LEN, D_MODEL)
    assert ffn1_w_in.shape[1:] == (D_MODEL, 2 * D_FF) and w_out.shape[1:] == (D_MODEL, D_MODEL)
    assert s % ATT_TQ == 0 and s % MIX_TM == 0 and (b * s) % FFN_TM == 0
    assert MIX_TM % CHUNK == 0 and gmlp_w_s.shape[1:] == (TOK_HEADS, CHUNK, CHUNK)
    for i in range(depth):
        kind, j = i % 2, i // 2
        x = _ffn(x.reshape(b * s, d), norm_ffn1[i], ffn1_w_in[i], ffn1_w_out[i]).reshape(b, s, d)
        km, vm = _mem_kv(mem, mem_norm, mem_w_kv[i], mem_k_norm[i])
        if kind == 0:
            q, k, v, mq = _fox_in(x, norm_mix[i], fox_w_in[j], fox_b_f[j], fox_q_norm[j],
                                  fox_k_norm[j], mem_q_norm[i])
            tok = _fox_attn(q, k, v)
        else:
            tok, mq = _gmlp_in(x, norm_mix[i], gmlp_w_in[j], gmlp_v_norm[j], gmlp_w_s[j],
                               gmlp_b_s[j], mem_q_norm[i])
        x = _mix_ffn(x, tok, mq, km, vm, w_out[i], norm_ffn2[i], ffn2_w_in[i], ffn2_w_out[i])
    return x
```
